```python
import jax
import jax.numpy as jnp
from jax import lax
import numpy as np

D_MODEL = 2048
BATCH = 8
SEQ = 2048
DEPTH = 4
DEC_BATCH = 8
DEC_SEQ = 64
PAST_LEN = 1024

CHUNK = 64
PLE_DIM = 256
POOL_WINDOWS = (2, 4, 8, 16)
POOL_GROUPS = 4
POOL_WIDTH = D_MODEL // 2
POOL_GC = POOL_WIDTH // POOL_GROUPS
POOL_PAST = max(POOL_WINDOWS) - 1
RW_WIDTH = D_MODEL // 2
RW_HEAD = 64
RW_HEADS = RW_WIDTH // RW_HEAD
DECAY_LORA = 64
AAA_LORA = 64
GATE_LORA = 160
RW_PROJ = 3 * RW_WIDTH + DECAY_LORA + AAA_LORA + GATE_LORA
IN_COLS = POOL_WIDTH + RW_PROJ + 2 * D_MODEL
D_FF = ((8 * D_MODEL + 3 * 256 - 1) // (3 * 256)) * 256
NORM_EPS = 1e-6
GN_EPS = 64e-5

kernel_name = 'hybrid_pool_rwkv7_stream_step'


def rmsnorm(x, g):
    xf = x.astype(jnp.float32)
    y = xf * lax.rsqrt(jnp.mean(xf * xf, axis=-1, keepdims=True) + NORM_EPS)
    return (y * g.astype(jnp.float32)).astype(x.dtype)


def pool_mixer(u, past, pos0, pool_w, pool_scale):
    B, T, _ = u.shape
    z = jnp.concatenate([past.astype(u.dtype), u], axis=1)
    zf = z.astype(jnp.float32)
    c = jnp.concatenate([jnp.zeros((B, 1, POOL_WIDTH), jnp.float32), jnp.cumsum(zf, axis=1)], axis=1)
    pos = pos0 + jnp.arange(T, dtype=jnp.int32)
    means = []
    for gi, win in enumerate(POOL_WINDOWS):
        sl = slice(gi * POOL_GC, (gi + 1) * POOL_GC)
        hi = c[:, POOL_PAST + 1:POOL_PAST + 1 + T, sl]
        lo = c[:, POOL_PAST + 1 - win:POOL_PAST + 1 - win + T, sl]
        cnt = jnp.minimum(pos + 1, win).astype(jnp.float32)[None, :, None]
        means.append((hi - lo) / cnt)
    d = jnp.concatenate(means, axis=-1) - zf[:, POOL_PAST:]
    d = d.astype(u.dtype).reshape(B, T, POOL_GROUPS, POOL_GC)
    y = jnp.einsum('btgc,gcd->btgd', d, pool_w).reshape(B, T, POOL_WIDTH) * pool_scale
    return y, z[:, -POOL_PAST:]


def wkv7_scan(r, decay, k, v, kk, b, S0):
    def step(S, xs):
        r_t, w_t, k_t, v_t, kk_t, b_t = xs
        sa = jnp.einsum('bhvk,bhk->bhv', S, -kk_t)
        S = S * w_t[:, :, None, :] + sa[..., None] * b_t[:, :, None, :] + v_t[..., None] * k_t[:, :, None, :]
        y = jnp.einsum('bhvk,bhk->bhv', S, r_t)
        return S, y
    xs = tuple(jnp.moveaxis(a, 1, 0) for a in (r, decay, k, v, kk, b))
    S, ys = lax.scan(step, S0, xs)
    return jnp.moveaxis(ys, 0, 1), S


def rwkv7_mix(rw, shift_prev, S0, mu, w0, w_decay_up, a0, w_aaa_up, w_gate_up, k_k, k_a, r_k, gn_gain, gn_bias):
    B, T, _ = rw.shape
    prev = jnp.concatenate([shift_prev[:, None].astype(rw.dtype), rw[:, :-1]], axis=1)
    xm = rw + (prev - rw) * mu
    r, k, v, wd, ad, gd = jnp.split(
        xm, [RW_WIDTH, 2 * RW_WIDTH, 3 * RW_WIDTH, 3 * RW_WIDTH + DECAY_LORA,
             3 * RW_WIDTH + DECAY_LORA + AAA_LORA], axis=-1)
    w = -jax.nn.softplus(-(w0 + jnp.tanh(wd) @ w_decay_up)) - 0.5
    decay = jnp.exp(-jnp.exp(w.astype(jnp.float32)))
    a = jax.nn.sigmoid(a0 + ad @ w_aaa_up)
    g = jax.nn.sigmoid(gd) @ w_gate_up
    heads = lambda t: t.astype(jnp.float32).reshape(B, T, RW_HEADS, RW_HEAD)
    kk = heads(k * k_k)
    kk = kk / jnp.maximum(jnp.sqrt(jnp.sum(kk * kk, axis=-1, keepdims=True)), 1e-12)
    aH = heads(a)
    kH = heads(k * (1.0 + (a - 1.0) * k_a))
    rH = heads(r)
    vH = heads(v)
    y, S = wkv7_scan(rH, heads(decay), kH, vH, kk, kk * aH, S0.astype(jnp.float32))
    mean = jnp.mean(y, axis=-1, keepdims=True)
    var = jnp.mean(jnp.square(y - mean), axis=-1, keepdims=True)
    y = ((y - mean) * lax.rsqrt(var + GN_EPS)).reshape(B, T, RW_WIDTH)
    y = y * gn_gain.astype(jnp.float32) + gn_bias.astype(jnp.float32)
    bonus = (jnp.sum(rH * kH * r_k.astype(jnp.float32), axis=-1, keepdims=True) * vH).reshape(B, T, RW_WIDTH)
    y = (y + bonus) * g.astype(jnp.float32)
    return y.astype(rw.dtype), rw[:, -1], S.astype(S0.dtype)


def trunk_layer(x, p, shift_prev, pool_past, S0, pos0,
                norm_mix, w_in, mu_shift, pool_w, pool_scale, w0, w_decay_up, a0, w_aaa_up, w_gate_up,
                k_k, k_a, r_k, gn_gain, gn_bias, proj_pool, proj_rwkv, w_out,
                norm_ffn, w_ffn_gate, w_ffn_up, w_ffn_down, norm_ple, w_ple_gate, w_ple_proj):
    h = rmsnorm(x, norm_mix)
    proj = h @ w_in
    u_pool, u_rw, g_pool, g_rw = jnp.split(
        proj, [POOL_WIDTH, POOL_WIDTH + RW_PROJ, POOL_WIDTH + RW_PROJ + D_MODEL], axis=-1)
    y_a, new_pool = pool_mixer(u_pool, pool_past, pos0, pool_w, pool_scale)
    y_b, new_shift, S = rwkv7_mix(u_rw, shift_prev, S0, mu_shift, w0, w_decay_up, a0, w_aaa_up,
                                  w_gate_up, k_k, k_a, r_k, gn_gain, gn_bias)
    merged = jax.nn.sigmoid(g_pool) * (y_a @ proj_pool) + jax.nn.sigmoid(g_rw) * (y_b @ proj_rwkv)
    x = x + merged @ w_out
    h2 = rmsnorm(x, norm_ffn)
    x = x + (jax.nn.silu(h2 @ w_ffn_gate) * (h2 @ w_ffn_up)) @ w_ffn_down
    h3 = rmsnorm(x, norm_ple)
    x = x + jax.nn.sigmoid(h3 @ w_ple_gate) * (p @ w_ple_proj)
    return x, new_shift, new_pool, S


def setup_inputs(seed: int = 0) -> dict:
    key = jax.random.key(seed)
    ks = iter(jax.random.split(key, 40))
    nrm = lambda shape, scale: jax.random.normal(next(ks), shape, jnp.float32) * scale
    unif = lambda shape, lo, hi: jax.random.uniform(next(ks), shape, jnp.float32, lo, hi)
    L = DEPTH
    return {
        'x_prompt': nrm((BATCH, SEQ, D_MODEL), 1.0),
        'x_sample': nrm((DEC_BATCH, DEC_SEQ, D_MODEL), 1.0),
        'state_shift': nrm((L, DEC_BATCH, RW_PROJ), 1.0),
        'state_pool': nrm((L, DEC_BATCH, POOL_PAST, POOL_WIDTH), 1.0),
        'state_wkv': nrm((L, DEC_BATCH, RW_HEADS, RW_HEAD, RW_HEAD), 0.5),
        'p_prompt': nrm((L, BATCH, SEQ, PLE_DIM), 1.0),
        'p_sample': nrm((L, DEC_BATCH, DEC_SEQ, PLE_DIM), 1.0),
        'norm_mix': 1.0 + nrm((L, D_MODEL), 0.05),
        'w_in': nrm((L, D_MODEL, IN_COLS), D_MODEL ** -0.5),
        'mu_shift': unif((L, RW_PROJ), 0.0, 1.0),
        'pool_w': nrm((L, POOL_GROUPS, POOL_GC, POOL_GC), POOL_GC ** -0.5),
        'pool_scale': 1.0 + nrm((L, POOL_WIDTH), 0.1),
        'w0': unif((L, RW_WIDTH), -5.0, 0.5),
        'w_decay_up': nrm((L, DECAY_LORA, RW_WIDTH), 0.1 * DECAY_LORA ** -0.5),
        'a0': nrm((L, RW_WIDTH), 0.5),
        'w_aaa_up': nrm((L, AAA_LORA, RW_WIDTH), AAA_LORA ** -0.5),
        'w_gate_up': nrm((L, GATE_LORA, RW_WIDTH), GATE_LORA ** -0.5),
        'k_k': 0.85 + nrm((L, RW_WIDTH), 0.05),
        'k_a': 1.0 + nrm((L, RW_WIDTH), 0.05),
        'r_k': nrm((L, RW_HEADS, RW_HEAD), 0.1),
        'gn_gain': 1.0 + nrm((L, RW_WIDTH), 0.05),
        'gn_bias': nrm((L, RW_WIDTH), 0.01),
        'proj_pool': nrm((L, POOL_WIDTH, D_MODEL), POOL_WIDTH ** -0.5),
        'proj_rwkv': nrm((L, RW_WIDTH, D_MODEL), RW_WIDTH ** -0.5),
        'w_out': nrm((L, D_MODEL, D_MODEL), D_MODEL ** -0.5),
        'norm_ffn': 1.0 + nrm((L, D_MODEL), 0.05),
        'w_ffn_gate': nrm((L, D_MODEL, D_FF), D_MODEL ** -0.5),
        'w_ffn_up': nrm((L, D_MODEL, D_FF), D_MODEL ** -0.5),
        'w_ffn_down': nrm((L, D_FF, D_MODEL), D_FF ** -0.5),
        'norm_ple': 1.0 + nrm((L, D_MODEL), 0.05),
        'w_ple_gate': nrm((L, D_MODEL, D_MODEL), D_MODEL ** -0.5),
        'w_ple_proj': nrm((L, PLE_DIM, D_MODEL), PLE_DIM ** -0.5),
        'norm_final': 1.0 + nrm((D_MODEL,), 0.05),
    }


def reference(x_prompt, x_sample, state_shift, state_pool, state_wkv, p_prompt, p_sample,
              norm_mix, w_in, mu_shift, pool_w, pool_scale, w0, w_decay_up, a0, w_aaa_up, w_gate_up,
              k_k, k_a, r_k, gn_gain, gn_bias, proj_pool, proj_rwkv, w_out,
              norm_ffn, w_ffn_gate, w_ffn_up, w_ffn_down, norm_ple, w_ple_gate, w_ple_proj, norm_final):
    def layer_params(i):
        return (norm_mix[i], w_in[i], mu_shift[i], pool_w[i], pool_scale[i], w0[i], w_decay_up[i], a0[i],
                w_aaa_up[i], w_gate_up[i], k_k[i], k_a[i], r_k[i], gn_gain[i], gn_bias[i], proj_pool[i],
                proj_rwkv[i], w_out[i], norm_ffn[i], w_ffn_gate[i], w_ffn_up[i], w_ffn_down[i], norm_ple[i],
                w_ple_gate[i], w_ple_proj[i])

    bp = x_prompt.shape[0]
    x = x_prompt
    sh_p, po_p, wk_p = [], [], []
    for i in range(DEPTH):
        x, s_sh, s_po, s_wk = trunk_layer(
            x, p_prompt[i], jnp.zeros((bp, RW_PROJ), x.dtype), jnp.zeros((bp, POOL_PAST, POOL_WIDTH), x.dtype),
            jnp.zeros((bp, RW_HEADS, RW_HEAD, RW_HEAD), jnp.float32), 0, *layer_params(i))
        sh_p.append(s_sh)
        po_p.append(s_po)
        wk_p.append(s_wk)
    y_prompt = rmsnorm(x, norm_final)

    x = x_sample
    sh_s, po_s, wk_s = [], [], []
    for i in range(DEPTH):
        x, s_sh, s_po, s_wk = trunk_layer(
            x, p_sample[i], state_shift[i], state_pool[i], state_wkv[i], PAST_LEN, *layer_params(i))
        sh_s.append(s_sh)
        po_s.append(s_po)
        wk_s.append(s_wk)
    y_sample = rmsnorm(x, norm_final)

    return (y_prompt, y_sample,
            jnp.stack(sh_p), jnp.stack(po_p), jnp.stack(wk_p),
            jnp.stack(sh_s), jnp.stack(po_s), jnp.stack(wk_s))
```

```python
import functools

import jax
import jax.numpy as jnp
from jax import lax
from jax.experimental import pallas as pl
from jax.experimental.pallas import tpu as pltpu

F32 = jnp.float32
BF16 = jnp.bfloat16

D_MODEL = 2048
DEPTH = 4
PAST_LEN = 1024
PLE_DIM = 256
POOL_WINDOWS = (2, 4, 8, 16)
POOL_GROUPS = 4
POOL_WIDTH = D_MODEL // 2
POOL_GC = POOL_WIDTH // POOL_GROUPS
POOL_PAST = max(POOL_WINDOWS) - 1
RW_WIDTH = D_MODEL // 2
RW_HEAD = 64
RW_HEADS = RW_WIDTH // RW_HEAD
DECAY_LORA = 64
AAA_LORA = 64
GATE_LORA = 160
LORA = DECAY_LORA + AAA_LORA + GATE_LORA
RW_PROJ = 3 * RW_WIDTH + LORA
D_FF = 5632
NORM_EPS = 1e-6
GN_EPS = 64e-5

LANES = 128
POOL_PAD = 16
LORA_PAD = 512
LORA_GATE_PAD = 256
COL_RKV = 0
COL_POOL = 3 * RW_WIDTH
COL_GPOOL = COL_POOL + POOL_WIDTH
COL_GRW = COL_GPOOL + D_MODEL
COL_LORA = COL_GRW + D_MODEL
IN_COLS_PAD = COL_LORA + LORA_PAD

TM = 512
TN = 512
VMEM_LIMIT = 48 * 1024 * 1024


def _params(sem):
    return pltpu.CompilerParams(dimension_semantics=sem, vmem_limit_bytes=VMEM_LIMIT)


def _rms(x, g):
    ms = jnp.mean(x * x, axis=-1, keepdims=True)
    return x * lax.rsqrt(ms + NORM_EPS) * g


def _norm_mm_kernel(x_ref, g_ref, w_ref, o_ref, h_scr):
    @pl.when(pl.program_id(1) == 0)
    def _():
        h_scr[...] = _rms(x_ref[...], g_ref[...]).astype(BF16)

    o_ref[...] = jnp.dot(h_scr[...], w_ref[...], preferred_element_type=F32)


def norm_matmul(x, g, w):
    m, k = x.shape
    n = w.shape[1]
    return pl.pallas_call(
        _norm_mm_kernel,
        out_shape=jax.ShapeDtypeStruct((m, n), F32),
        grid=(m // TM, n // TN),
        in_specs=[
            pl.BlockSpec((TM, k), lambda i, j: (i, 0)),
            pl.BlockSpec((1, k), lambda i, j: (0, 0)),
            pl.BlockSpec((k, TN), lambda i, j: (0, j)),
        ],
        out_specs=pl.BlockSpec((TM, TN), lambda i, j: (i, j)),
        scratch_shapes=[pltpu.VMEM((TM, k), BF16)],
        compiler_params=_params(("parallel", "arbitrary")),
        name="norm_matmul",
    )(x, g.reshape(1, k), w)


def _mm_res_kernel(a_ref, w_ref, r_ref, o_ref):
    o_ref[...] = r_ref[...] + jnp.dot(a_ref[...], w_ref[...], preferred_element_type=F32)


def matmul_residual(a, w, res):
    m, k = a.shape
    n = w.shape[1]
    return pl.pallas_call(
        _mm_res_kernel,
        out_shape=jax.ShapeDtypeStruct((m, n), F32),
        grid=(m // TM, n // TN),
        in_specs=[
            pl.BlockSpec((TM, k), lambda i, j: (i, 0)),
            pl.BlockSpec((k, TN), lambda i, j: (0, j)),
            pl.BlockSpec((TM, TN), lambda i, j: (i, j)),
        ],
        out_specs=pl.BlockSpec((TM, TN), lambda i, j: (i, j)),
        compiler_params=_params(("parallel", "arbitrary")),
        name="matmul_residual",
    )(a, w, res)


def _ffn_up_kernel(x_ref, g_ref, wg_ref, wu_ref, o_ref, h_scr):
    @pl.when(pl.program_id(1) == 0)
    def _():
        h_scr[...] = _rms(x_ref[...], g_ref[...]).astype(BF16)

    h = h_scr[...]
    gate = jnp.dot(h, wg_ref[...], preferred_element_type=F32)
    up = jnp.dot(h, wu_ref[...], preferred_element_type=F32)
    o_ref[...] = (gate * jax.nn.sigmoid(gate) * up).astype(BF16)


def ffn_up(x, g, wg, wu):
    m, k = x.shape
    n = wg.shape[1]
    return pl.pallas_call(
        _ffn_up_kernel,
        out_shape=jax.ShapeDtypeStruct((m, n), BF16),
        grid=(m // TM, n // TN),
        in_specs=[
            pl.BlockSpec((TM, k), lambda i, j: (i, 0)),
            pl.BlockSpec((1, k), lambda i, j: (0, 0)),
            pl.BlockSpec((k, TN), lambda i, j: (0, j)),
            pl.BlockSpec((k, TN), lambda i, j: (0, j)),
        ],
        out_specs=pl.BlockSpec((TM, TN), lambda i, j: (i, j)),
        scratch_shapes=[pltpu.VMEM((TM, k), BF16)],
        compiler_params=_params(("parallel", "arbitrary")),
        name="ffn_up",
    )(x, g.reshape(1, k), wg, wu)


def _ple_kernel(x_ref, g_ref, wg_ref, p_ref, wp_ref, xr_ref, o_ref, h_scr):
    @pl.when(pl.program_id(1) == 0)
    def _():
        h_scr[...] = _rms(x_ref[...], g_ref[...]).astype(BF16)

    gate = jnp.dot(h_scr[...], wg_ref[...], preferred_element_type=F32)
    emb = jnp.dot(p_ref[...].astype(BF16), wp_ref[...], preferred_element_type=F32)
    o_ref[...] = xr_ref[...] + jax.nn.sigmoid(gate) * emb


def ple_update(x, g, wg, p, wp):
    m, k = x.shape
    n = wg.shape[1]
    kp = p.shape[1]
    return pl.pallas_call(
        _ple_kernel,
        out_shape=jax.ShapeDtypeStruct((m, n), F32),
        grid=(m // TM, n // TN),
        in_specs=[
            pl.BlockSpec((TM, k), lambda i, j: (i, 0)),
            pl.BlockSpec((1, k), lambda i, j: (0, 0)),
            pl.BlockSpec((k, TN), lambda i, j: (0, j)),
            pl.BlockSpec((TM, kp), lambda i, j: (i, 0)),
            pl.BlockSpec((kp, TN), lambda i, j: (0, j)),
            pl.BlockSpec((TM, TN), lambda i, j: (i, j)),
        ],
        out_specs=pl.BlockSpec((TM, TN), lambda i, j: (i, j)),
        scratch_shapes=[pltpu.VMEM((TM, k), BF16)],
        compiler_params=_params(("parallel", "arbitrary")),
        name="ple_update",
    )(x, g.reshape(1, k), wg, p, wp, x)


def _merge_kernel(ya_ref, yb_ref, g_ref, gp_ref, gr_ref, wp_ref, wr_ref, o_ref, ybg_scr):
    @pl.when(pl.program_id(1) == 0)
    def _():
        ybg_scr[...] = (yb_ref[...] * g_ref[...]).astype(BF16)

    a = jnp.dot(ya_ref[...], wp_ref[...], preferred_element_type=F32)
    b = jnp.dot(ybg_scr[...], wr_ref[...], preferred_element_type=F32)
    o_ref[...] = (jax.nn.sigmoid(gp_ref[...]) * a + jax.nn.sigmoid(gr_ref[...]) * b).astype(BF16)


def merge_branches(ya, yb, g, proj, wp, wr):
    m = ya.shape[0]
    n = wp.shape[1]
    gp0 = COL_GPOOL // TN
    gr0 = COL_GRW // TN
    return pl.pallas_call(
        _merge_kernel,
        out_shape=jax.ShapeDtypeStruct((m, n), BF16),
        grid=(m // TM, n // TN),
        in_specs=[
            pl.BlockSpec((TM, POOL_WIDTH), lambda i, j: (i, 0)),
            pl.BlockSpec((TM, RW_WIDTH), lambda i, j: (i, 0)),
            pl.BlockSpec((TM, RW_WIDTH), lambda i, j: (i, 0)),
            pl.BlockSpec((TM, TN), lambda i, j: (i, gp0 + j)),
            pl.BlockSpec((TM, TN), lambda i, j: (i, gr0 + j)),
            pl.BlockSpec((POOL_WIDTH, TN), lambda i, j: (0, j)),
            pl.BlockSpec((RW_WIDTH, TN), lambda i, j: (0, j)),
        ],
        out_specs=pl.BlockSpec((TM, TN), lambda i, j: (i, j)),
        scratch_shapes=[pltpu.VMEM((TM, RW_WIDTH), BF16)],
        compiler_params=_params(("parallel", "arbitrary")),
        name="merge_branches",
    )(ya, yb, g, proj, proj, wp, wr)


def _final_norm_kernel(x_ref, g_ref, o_ref):
    o_ref[...] = _rms(x_ref[...], g_ref[...])


def final_norm(x, g):
    m, k = x.shape
    return pl.pallas_call(
        _final_norm_kernel,
        out_shape=jax.ShapeDtypeStruct((m, k), F32),
        grid=(m // TM,),
        in_specs=[pl.BlockSpec((TM, k), lambda i: (i, 0)), pl.BlockSpec((1, k), lambda i: (0, 0))],
        out_specs=pl.BlockSpec((TM, k), lambda i: (i, 0)),
        compiler_params=_params(("parallel",)),
        name="final_norm",
    )(x, g.reshape(1, k))


def _seq_tile(t_len):
    return min(t_len, 256)


def _pool_kernel(u_ref, past_ref, pw_ref, ps_ref, o_ref, carry, *, tt, pos0):
    t = pl.program_id(1)

    @pl.when(t == 0)
    def _():
        carry[...] = past_ref[0]

    u = u_ref[...]
    ext = jnp.concatenate([carry[...], u], axis=0)
    carry[...] = ext[tt:tt + POOL_PAD]
    pos = pos0 + t * tt + lax.broadcasted_iota(jnp.int32, (tt, 1), 0)
    outs = []
    for gi, win in enumerate(POOL_WINDOWS):
        sl = slice(gi * POOL_GC, (gi + 1) * POOL_GC)
        s = ext[:, sl]
        sh = 1
        while sh < win:
            s = s + pltpu.roll(s, sh, axis=0)
            sh *= 2
        cnt = jnp.minimum(pos + 1, win).astype(F32)
        d = s[POOL_PAD:] / cnt - u[:, sl]
        outs.append(jnp.dot(d.astype(BF16), pw_ref[gi], preferred_element_type=F32))
    o_ref[...] = (jnp.concatenate(outs, axis=-1) * ps_ref[...]).astype(BF16)


def pool_mixer(proj, past, pool_w, pool_scale, *, row0, batch, t_len, pos0):
    tt = _seq_tile(t_len)
    nt = t_len // tt
    rb0 = row0 // tt
    cb = COL_POOL // POOL_WIDTH
    return pl.pallas_call(
        functools.partial(_pool_kernel, tt=tt, pos0=pos0),
        out_shape=jax.ShapeDtypeStruct((batch * t_len, POOL_WIDTH), BF16),
        grid=(batch, nt),
        in_specs=[
            pl.BlockSpec((tt, POOL_WIDTH), lambda b, t: (rb0 + b * nt + t, cb)),
            pl.BlockSpec((1, POOL_PAD, POOL_WIDTH), lambda b, t: (b, 0, 0)),
            pl.BlockSpec((POOL_GROUPS, POOL_GC, POOL_GC), lambda b, t: (0, 0, 0)),
            pl.BlockSpec((1, POOL_WIDTH), lambda b, t: (0, 0)),
        ],
        out_specs=pl.BlockSpec((tt, POOL_WIDTH), lambda b, t: (b * nt + t, 0)),
        scratch_shapes=[pltpu.VMEM((POOL_PAD, POOL_WIDTH), F32)],
        compiler_params=_params(("parallel", "arbitrary")),
        name="pool_mixer",
    )(proj, past, pool_w, pool_scale.reshape(1, POOL_WIDTH))


def _shift_mix(cur, carry_row, mu):
    rolled = pltpu.roll(cur, 1, axis=0)
    row = lax.broadcasted_iota(jnp.int32, cur.shape, 0)
    prev = jnp.where(row == 0, carry_row, rolled)
    return cur + (prev - cur) * mu


def _softplus(x):
    return jnp.maximum(x, 0.0) + jnp.log1p(jnp.exp(-jnp.abs(x)))


def _rw_prep_kernel(r_ref, k_ref, v_ref, l_ref, sp_ref, spl_ref, mu_ref, mul_ref,
                    w0_ref, a0_ref, kk_ref, ka_ref, wd_ref, wa_ref, wg_ref, seg_ref,
                    pre_ref, g_ref, carry, carry_l, *, tt):
    t = pl.program_id(1)

    @pl.when(t == 0)
    def _():
        carry[0:3, :] = sp_ref[0]
        carry_l[0:1, :] = spl_ref[0]

    r = r_ref[...]
    k = k_ref[...]
    v = v_ref[...]
    lo = l_ref[...]
    xr = _shift_mix(r, carry[0:1, :], mu_ref[0:1, :])
    xk = _shift_mix(k, carry[1:2, :], mu_ref[1:2, :])
    xv = _shift_mix(v, carry[2:3, :], mu_ref[2:3, :])
    xl = _shift_mix(lo, carry_l[0:1, :], mul_ref[...])
    carry[0:1, :] = r[tt - 1:tt, :]
    carry[1:2, :] = k[tt - 1:tt, :]
    carry[2:3, :] = v[tt - 1:tt, :]
    carry_l[0:1, :] = lo[tt - 1:tt, :]

    xda = xl[:, 0:LANES]
    xg = xl[:, LANES:LANES + LORA_GATE_PAD]
    zd = jnp.dot(jnp.tanh(xda).astype(BF16), wd_ref[...], preferred_element_type=F32)
    w = -_softplus(-(w0_ref[...] + zd)) - 0.5
    decay = jnp.exp(-jnp.exp(w))
    za = jnp.dot(xda.astype(BF16), wa_ref[...], preferred_element_type=F32)
    a = jax.nn.sigmoid(a0_ref[...] + za)
    g_ref[...] = jnp.dot(jax.nn.sigmoid(xg).astype(BF16), wg_ref[...], preferred_element_type=F32)

    kkr = xk * kk_ref[...]
    sq = kkr * kkr
    sq_hi = sq.astype(BF16)
    sq_lo = (sq - sq_hi.astype(F32)).astype(BF16)
    seg = seg_ref[...]
    parts = []
    for c in range(RW_WIDTH // LANES):
        sl = slice(c * LANES, (c + 1) * LANES)
        parts.append(jnp.dot(sq_hi[:, sl], seg, preferred_element_type=F32)
                     + jnp.dot(sq_lo[:, sl], seg, preferred_element_type=F32))
    ss = jnp.concatenate(parts, axis=-1)
    kk = kkr / jnp.maximum(jnp.sqrt(ss), 1e-12)

    pre_ref[0] = xr
    pre_ref[1] = decay
    pre_ref[2] = xk * (1.0 + (a - 1.0) * ka_ref[...])
    pre_ref[3] = xv
    pre_ref[4] = kk
    pre_ref[5] = kk * a


def rw_prep(proj, sp_rkv, sp_lora, lw, *, row0, batch, t_len):
    tt = _seq_tile(t_len)
    nt = t_len // tt
    rb0 = row0 // tt
    lb = COL_LORA // LORA_PAD
    m = batch * t_len
    row = lambda c: pl.BlockSpec((tt, RW_WIDTH), lambda b, t: (rb0 + b * nt + t, c))
    full = lambda shape: pl.BlockSpec(shape, lambda b, t: tuple(0 for _ in shape))
    return pl.pallas_call(
        functools.partial(_rw_prep_kernel, tt=tt),
        out_shape=(jax.ShapeDtypeStruct((6, m, RW_WIDTH), F32),
                   jax.ShapeDtypeStruct((m, RW_WIDTH), F32)),
        grid=(batch, nt),
        in_specs=[
            row(0), row(1), row(2),
            pl.BlockSpec((tt, LORA_PAD), lambda b, t: (rb0 + b * nt + t, lb)),
            pl.BlockSpec((1, 3, RW_WIDTH), lambda b, t: (b, 0, 0)),
            pl.BlockSpec((1, 1, LORA_PAD), lambda b, t: (b, 0, 0)),
            full((3, RW_WIDTH)), full((1, LORA_PAD)),
            full((1, RW_WIDTH)), full((1, RW_WIDTH)), full((1, RW_WIDTH)), full((1, RW_WIDTH)),
            full((LANES, RW_WIDTH)), full((LANES, RW_WIDTH)), full((LORA_GATE_PAD, RW_WIDTH)),
            full((LANES, LANES)),
        ],
        out_specs=(pl.BlockSpec((6, tt, RW_WIDTH), lambda b, t: (0, b * nt + t, 0)),
                   pl.BlockSpec((tt, RW_WIDTH), lambda b, t: (b * nt + t, 0))),
        scratch_shapes=[pltpu.VMEM((8, RW_WIDTH), F32), pltpu.VMEM((8, LORA_PAD), F32)],
        compiler_params=_params(("parallel", "arbitrary")),
        name="rw_prep",
    )(proj, proj, proj, proj, sp_rkv, sp_lora, lw["mu_rkv"], lw["mu_lora"],
      lw["w0"], lw["a0"], lw["k_k"], lw["k_a"], lw["wd"], lw["wa"], lw["wg"], lw["seg"])


def _wkv_kernel(xs_ref, s0_ref, gain_ref, bias_ref, rk_ref, y_ref, s_ref, *, tc):
    @pl.when(pl.program_id(0) == 0)
    def _():
        s_ref[...] = s0_ref[...]

    def step(t, carry):
        acc = [jnp.zeros((RW_HEAD, LANES), F32) for _ in range(2)]
        for k in range(RW_HEAD):
            acc[k % 2] = acc[k % 2] + s_ref[k] * xs_ref[t, 4, k:k + 1, :]
        sa = -(acc[0] + acc[1])
        vv = xs_ref[t, 3]
        yacc = [jnp.zeros((RW_HEAD, LANES), F32) for _ in range(2)]
        for k in range(RW_HEAD):
            sn = (s_ref[k] * xs_ref[t, 1, k:k + 1, :] + sa * xs_ref[t, 5, k:k + 1, :]
                  + vv * xs_ref[t, 2, k:k + 1, :])
            s_ref[k] = sn
            yacc[k % 2] = yacc[k % 2] + sn * xs_ref[t, 0, k:k + 1, :]
        y = yacc[0] + yacc[1]
        mean = jnp.mean(y, axis=0, keepdims=True)
        yc = y - mean
        var = jnp.mean(yc * yc, axis=0, keepdims=True)
        yn = yc * lax.rsqrt(var + GN_EPS) * gain_ref[...] + bias_ref[...]
        bonus = jnp.sum(xs_ref[t, 0] * xs_ref[t, 2] * rk_ref[...], axis=0, keepdims=True)
        y_ref[t] = yn + bonus * vv
        return carry

    lax.fori_loop(0, tc, step, 0)


def wkv_scan(xs, s0, gain_t, bias_t, rk_t):
    t_len = xs.shape[0]
    tc = 16
    tab = pl.BlockSpec((RW_HEAD, LANES), lambda i: (0, 0))
    st = pl.BlockSpec((RW_HEAD, RW_HEAD, LANES), lambda i: (0, 0, 0))
    return pl.pallas_call(
        functools.partial(_wkv_kernel, tc=tc),
        out_shape=(jax.ShapeDtypeStruct((t_len, RW_HEAD, LANES), F32),
                   jax.ShapeDtypeStruct((RW_HEAD, RW_HEAD, LANES), F32)),
        grid=(t_len // tc,),
        in_specs=[pl.BlockSpec((tc, 6, RW_HEAD, LANES), lambda i: (i, 0, 0, 0)), st, tab, tab, tab],
        out_specs=(pl.BlockSpec((tc, RW_HEAD, LANES), lambda i: (i, 0, 0)), st),
        compiler_params=_params(("arbitrary",)),
        name="wkv_scan",
    )(xs, s0, gain_t, bias_t, rk_t)


def _head_table(vec, batch):
    tab = vec.reshape(RW_HEADS, RW_HEAD).T
    return jnp.tile(tab, (1, batch))


def _to_scan_layout(pre, batch, t_len):
    x = pre.reshape(6, batch, t_len, RW_HEADS, RW_HEAD)
    return x.transpose(2, 0, 4, 1, 3).reshape(t_len, 6, RW_HEAD, batch * RW_HEADS)


def _from_scan_layout(y, batch, t_len):
    x = y.reshape(t_len, RW_HEAD, batch, RW_HEADS)
    return x.transpose(2, 0, 3, 1).reshape(batch * t_len, RW_WIDTH)


def _layer_weights(i, norm_mix, w_in, mu_shift, pool_w, pool_scale, w0, w_decay_up, a0, w_aaa_up, w_gate_up,
                   k_k, k_a, r_k, gn_gain, gn_bias, proj_pool, proj_rwkv, w_out,
                   norm_ffn, w_ffn_gate, w_ffn_up, w_ffn_down, norm_ple, w_ple_gate, w_ple_proj):
    wi = w_in[i]
    c_rw = POOL_WIDTH
    c_lora = POOL_WIDTH + 3 * RW_WIDTH
    c_gp = POOL_WIDTH + RW_PROJ
    c_gr = c_gp + D_MODEL
    w_in_r = jnp.concatenate([
        wi[:, c_rw:c_lora], wi[:, :POOL_WIDTH], wi[:, c_gp:c_gr], wi[:, c_gr:], wi[:, c_lora:c_gp],
        jnp.zeros((D_MODEL, LORA_PAD - LORA), F32)], axis=1).astype(BF16)
    mu = mu_shift[i]
    zpad = lambda rows: jnp.zeros((rows, RW_WIDTH), F32)
    wd = jnp.concatenate([w_decay_up[i], zpad(LANES - DECAY_LORA)], axis=0).astype(BF16)
    wa = jnp.concatenate([zpad(DECAY_LORA), w_aaa_up[i]], axis=0).astype(BF16)
    wg = jnp.concatenate([w_gate_up[i], zpad(LORA_GATE_PAD - GATE_LORA)], axis=0).astype(BF16)
    lane = jnp.arange(LANES) // RW_HEAD
    seg = (lane[:, None] == lane[None, :]).astype(BF16)
    row = lambda v: v.reshape(1, -1)
    return dict(
        norm_mix=norm_mix[i], w_in=w_in_r,
        mu_rkv=mu[:3 * RW_WIDTH].reshape(3, RW_WIDTH),
        mu_lora=jnp.pad(mu[3 * RW_WIDTH:], (0, LORA_PAD - LORA)).reshape(1, LORA_PAD),
        pool_w=pool_w[i].astype(BF16), pool_scale=pool_scale[i],
        w0=row(w0[i]), a0=row(a0[i]), k_k=row(k_k[i]), k_a=row(k_a[i]), wd=wd, wa=wa, wg=wg, seg=seg,
        r_k=r_k[i].reshape(RW_WIDTH), gn_gain=gn_gain[i], gn_bias=gn_bias[i],
        proj_pool=proj_pool[i].astype(BF16), proj_rwkv=proj_rwkv[i].astype(BF16), w_out=w_out[i].astype(BF16),
        norm_ffn=norm_ffn[i], w_ffn_gate=w_ffn_gate[i].astype(BF16), w_ffn_up=w_ffn_up[i].astype(BF16),
        w_ffn_down=w_ffn_down[i].astype(BF16), norm_ple=norm_ple[i],
        w_ple_gate=w_ple_gate[i].astype(BF16), w_ple_proj=w_ple_proj[i].astype(BF16))


def _mixers(proj, lw, shift_prev, pool_past, s0, *, row0, batch, t_len, pos0):
    past = jnp.pad(pool_past, ((0, 0), (POOL_PAD - POOL_PAST, 0), (0, 0)))
    ya = pool_mixer(proj, past, lw["pool_w"], lw["pool_scale"], row0=row0, batch=batch, t_len=t_len, pos0=pos0)
    sp_rkv = shift_prev[:, :3 * RW_WIDTH].reshape(batch, 3, RW_WIDTH)
    sp_lora = jnp.pad(shift_prev[:, 3 * RW_WIDTH:], ((0, 0), (0, LORA_PAD - LORA))).reshape(batch, 1, LORA_PAD)
    pre, g = rw_prep(proj, sp_rkv, sp_lora, lw, row0=row0, batch=batch, t_len=t_len)
    xs = _to_scan_layout(pre, batch, t_len)
    s0_t = s0.transpose(3, 2, 0, 1).reshape(RW_HEAD, RW_HEAD, batch * RW_HEADS)
    y_t, s_t = wkv_scan(xs, s0_t, _head_table(lw["gn_gain"], batch), _head_table(lw["gn_bias"], batch),
                        _head_table(lw["r_k"], batch))
    yb = _from_scan_layout(y_t, batch, t_len)
    s_new = s_t.reshape(RW_HEAD, RW_HEAD, batch, RW_HEADS).transpose(2, 3, 1, 0)
    rows = proj[row0:row0 + batch * t_len].reshape(batch, t_len, IN_COLS_PAD)
    last = rows[:, t_len - 1]
    new_shift = jnp.concatenate([last[:, COL_RKV:COL_RKV + 3 * RW_WIDTH], last[:, COL_LORA:COL_LORA + LORA]], axis=1)
    new_pool = rows[:, t_len - POOL_PAST:, COL_POOL:COL_POOL + POOL_WIDTH]
    return ya, yb, g, new_shift, new_pool, s_new


def kernel(x_prompt, x_sample, state_shift, state_pool, state_wkv, p_prompt, p_sample, norm_mix, w_in, mu_shift, pool_w, pool_scale, w0, w_decay_up, a0, w_aaa_up, w_gate_up, k_k, k_a, r_k, gn_gain, gn_bias, proj_pool, proj_rwkv, w_out, norm_ffn, w_ffn_gate, w_ffn_up, w_ffn_down, norm_ple, w_ple_gate, w_ple_proj, norm_final):
    bp, tp, _ = x_prompt.shape
    bs, ts, _ = x_sample.shape
    mp = bp * tp
    ms = bs * ts
    x = jnp.concatenate([x_prompt.reshape(mp, D_MODEL), x_sample.reshape(ms, D_MODEL)], axis=0)
    zeros_shift = jnp.zeros((bp, RW_PROJ), F32)
    zeros_pool = jnp.zeros((bp, POOL_PAST, POOL_WIDTH), F32)
    zeros_wkv = jnp.zeros((bp, RW_HEADS, RW_HEAD, RW_HEAD), F32)
    outs_p, outs_s = [], []
    for i in range(DEPTH):
        lw = _layer_weights(i, norm_mix, w_in, mu_shift, pool_w, pool_scale, w0, w_decay_up, a0, w_aaa_up,
                            w_gate_up, k_k, k_a, r_k, gn_gain, gn_bias, proj_pool, proj_rwkv, w_out,
                            norm_ffn, w_ffn_gate, w_ffn_up, w_ffn_down, norm_ple, w_ple_gate, w_ple_proj)
        proj = norm_matmul(x, lw["norm_mix"], lw["w_in"])
        ya_p, yb_p, g_p, sh_p, po_p, wk_p = _mixers(
            proj, lw, zeros_shift, zeros_pool, zeros_wkv, row0=0, batch=bp, t_len=tp, pos0=0)
        ya_s, yb_s, g_s, sh_s, po_s, wk_s = _mixers(
            proj, lw, state_shift[i], state_pool[i], state_wkv[i], row0=mp, batch=bs, t_len=ts, pos0=PAST_LEN)
        outs_p.append((sh_p, po_p, wk_p))
        outs_s.append((sh_s, po_s, wk_s))
        ya = jnp.concatenate([ya_p, ya_s], axis=0)
        yb = jnp.concatenate([yb_p, yb_s], axis=0)
        g = jnp.concatenate([g_p, g_s], axis=0)
        merged = merge_branches(ya, yb, g, proj, lw["proj_pool"], lw["proj_rwkv"])
        x = matmul_residual(merged, lw["w_out"], x)
        act = ffn_up(x, lw["norm_ffn"], lw["w_ffn_gate"], lw["w_ffn_up"])
        x = matmul_residual(act, lw["w_ffn_down"], x)
        p = jnp.concatenate([p_prompt[i].reshape(mp, PLE_DIM), p_sample[i].reshape(ms, PLE_DIM)], axis=0)
        x = ple_update(x, lw["norm_ple"], lw["w_ple_gate"], p, lw["w_ple_proj"])
    y = final_norm(x, norm_final)
    y_prompt = y[:mp].reshape(bp, tp, D_MODEL)
    y_sample = y[mp:].reshape(bs, ts, D_MODEL)
    stack = lambda outs, j: jnp.stack([o[j] for o in outs])
    return (y_prompt, y_sample,
            stack(outs_p, 0), stack(outs_p, 1), stack(outs_p, 2),
            stack(outs_s, 0), stack(outs_s, 1), stack(outs_s, 2))
```

```python
import functools

import jax
import jax.numpy as jnp
from jax import lax
from jax.experimental import pallas as pl
from jax.experimental.pallas import tpu as pltpu

F32 = jnp.float32
BF16 = jnp.bfloat16

D_MODEL = 2048
DEPTH = 4
PAST_LEN = 1024
PLE_DIM = 256
POOL_WINDOWS = (2, 4, 8, 16)
POOL_GROUPS = 4
POOL_WIDTH = D_MODEL // 2
POOL_GC = POOL_WIDTH // POOL_GROUPS
POOL_PAST = max(POOL_WINDOWS) - 1
RW_WIDTH = D_MODEL // 2
RW_HEAD = 64
RW_HEADS = RW_WIDTH // RW_HEAD
DECAY_LORA = 64
AAA_LORA = 64
GATE_LORA = 160
LORA = DECAY_LORA + AAA_LORA + GATE_LORA
RW_PROJ = 3 * RW_WIDTH + LORA
D_FF = 5632
NORM_EPS = 1e-6
GN_EPS = 64e-5

LANES = 128
POOL_PAD = 16
LORA_PAD = 512
LORA_GATE_PAD = 256
COL_RKV = 0
COL_POOL = 3 * RW_WIDTH
COL_GPOOL = COL_POOL + POOL_WIDTH
COL_GRW = COL_GPOOL + D_MODEL
COL_LORA = COL_GRW + D_MODEL
IN_COLS_PAD = COL_LORA + LORA_PAD

TN = 512
TM_WIDE = 1536
TM_ROW = 768
VMEM_LIMIT = 56 * 1024 * 1024


def _params(sem):
    return pltpu.CompilerParams(dimension_semantics=sem, vmem_limit_bytes=VMEM_LIMIT)


def _rms(x, g):
    ms = jnp.mean(x * x, axis=-1, keepdims=True)
    return x * lax.rsqrt(ms + NORM_EPS) * g


def _wspec(k, n, layer, col_of):
    return pl.BlockSpec((None, k, n), lambda *idx: (layer, 0, col_of(*idx)))


def _norm_mm_kernel(x_ref, g_ref, w_ref, o_ref, h_scr):
    @pl.when(pl.program_id(1) == 0)
    def _():
        h_scr[...] = _rms(x_ref[...], g_ref[...]).astype(BF16)

    o_ref[...] = jnp.dot(h_scr[...], w_ref[...], preferred_element_type=F32)


def norm_matmul(x, g, w, layer):
    m, k = x.shape
    n = w.shape[2]
    tm = TM_WIDE
    return pl.pallas_call(
        _norm_mm_kernel,
        out_shape=jax.ShapeDtypeStruct((m, n), F32),
        grid=(m // tm, n // TN),
        in_specs=[
            pl.BlockSpec((tm, k), lambda i, j: (i, 0)),
            pl.BlockSpec((1, k), lambda i, j: (0, 0)),
            _wspec(k, TN, layer, lambda i, j: j),
        ],
        out_specs=pl.BlockSpec((tm, TN), lambda i, j: (i, j)),
        scratch_shapes=[pltpu.VMEM((tm, k), BF16)],
        compiler_params=_params(("parallel", "arbitrary")),
        name="norm_matmul",
    )(x, g.reshape(1, k), w)


def _out_proj_kernel(a_ref, w_ref, r_ref, o_ref):
    n = o_ref.shape[1]
    for c in range(n // TN):
        sl = slice(c * TN, (c + 1) * TN)
        o_ref[:, sl] = r_ref[:, sl] + jnp.dot(a_ref[...], w_ref[:, sl], preferred_element_type=F32)


def out_proj(a, w, res, layer):
    m, k = a.shape
    n = w.shape[2]
    tm = TM_ROW
    return pl.pallas_call(
        _out_proj_kernel,
        out_shape=jax.ShapeDtypeStruct((m, n), F32),
        grid=(m // tm,),
        in_specs=[
            pl.BlockSpec((tm, k), lambda i: (i, 0)),
            _wspec(k, n, layer, lambda i: 0),
            pl.BlockSpec((tm, n), lambda i: (i, 0)),
        ],
        out_specs=pl.BlockSpec((tm, n), lambda i: (i, 0)),
        compiler_params=_params(("parallel",)),
        name="out_proj",
    )(a, w, res)


def _ffn_kernel(x_ref, g_ref, wg_ref, wu_ref, wd_ref, o_ref, h_scr):
    @pl.when(pl.program_id(1) == 0)
    def _():
        x = x_ref[...]
        h_scr[...] = _rms(x, g_ref[...]).astype(BF16)
        o_ref[...] = x

    h = h_scr[...]
    gate = jnp.dot(h, wg_ref[...], preferred_element_type=F32)
    up = jnp.dot(h, wu_ref[...], preferred_element_type=F32)
    act = (gate * jax.nn.sigmoid(gate) * up).astype(BF16)
    o_ref[...] += jnp.dot(act, wd_ref[...], preferred_element_type=F32)


def ffn(x, g, wg, wu, wd, layer):
    m, k = x.shape
    f = wg.shape[2]
    tm = TM_ROW
    return pl.pallas_call(
        _ffn_kernel,
        out_shape=jax.ShapeDtypeStruct((m, k), F32),
        grid=(m // tm, f // TN),
        in_specs=[
            pl.BlockSpec((tm, k), lambda i, j: (i, 0)),
            pl.BlockSpec((1, k), lambda i, j: (0, 0)),
            _wspec(k, TN, layer, lambda i, j: j),
            _wspec(k, TN, layer, lambda i, j: j),
            pl.BlockSpec((None, TN, k), lambda i, j: (layer, j, 0)),
        ],
        out_specs=pl.BlockSpec((tm, k), lambda i, j: (i, 0)),
        scratch_shapes=[pltpu.VMEM((tm, k), BF16)],
        compiler_params=_params(("parallel", "arbitrary")),
        name="ffn",
    )(x, g.reshape(1, k), wg, wu, wd)


def _ple_kernel(x_ref, g_ref, wg_ref, p_ref, wp_ref, o_ref, h_scr):
    n = o_ref.shape[1]
    h_scr[...] = _rms(x_ref[...], g_ref[...]).astype(BF16)
    pb = p_ref[...].astype(BF16)
    for c in range(n // TN):
        sl = slice(c * TN, (c + 1) * TN)
        gate = jnp.dot(h_scr[...], wg_ref[:, sl], preferred_element_type=F32)
        emb = jnp.dot(pb, wp_ref[:, sl], preferred_element_type=F32)
        o_ref[:, sl] = x_ref[:, sl] + jax.nn.sigmoid(gate) * emb


def ple_update(x, g, wg, p, wp, layer):
    m, k = x.shape
    n = wg.shape[2]
    kp = p.shape[1]
    tm = TM_ROW
    return pl.pallas_call(
        _ple_kernel,
        out_shape=jax.ShapeDtypeStruct((m, n), F32),
        grid=(m // tm,),
        in_specs=[
            pl.BlockSpec((tm, k), lambda i: (i, 0)),
            pl.BlockSpec((1, k), lambda i: (0, 0)),
            _wspec(k, n, layer, lambda i: 0),
            pl.BlockSpec((tm, kp), lambda i: (i, 0)),
            _wspec(kp, n, layer, lambda i: 0),
        ],
        out_specs=pl.BlockSpec((tm, n), lambda i: (i, 0)),
        scratch_shapes=[pltpu.VMEM((tm, k), BF16)],
        compiler_params=_params(("parallel",)),
        name="ple_update",
    )(x, g.reshape(1, k), wg, p, wp)


def _merge_kernel(ya_ref, yb_ref, g_ref, gp_ref, gr_ref, wp_ref, wr_ref, o_ref, ybg_scr):
    @pl.when(pl.program_id(1) == 0)
    def _():
        ybg_scr[...] = (yb_ref[...] * g_ref[...]).astype(BF16)

    a = jnp.dot(ya_ref[...], wp_ref[...], preferred_element_type=F32)
    b = jnp.dot(ybg_scr[...], wr_ref[...], preferred_element_type=F32)
    o_ref[...] = (jax.nn.sigmoid(gp_ref[...]) * a + jax.nn.sigmoid(gr_ref[...]) * b).astype(BF16)


def merge_branches(ya, yb, g, proj, wp, wr, layer):
    m = ya.shape[0]
    n = wp.shape[2]
    tm = TM_ROW
    gp0 = COL_GPOOL // TN
    gr0 = COL_GRW // TN
    return pl.pallas_call(
        _merge_kernel,
        out_shape=jax.ShapeDtypeStruct((m, n), BF16),
        grid=(m // tm, n // TN),
        in_specs=[
            pl.BlockSpec((tm, POOL_WIDTH), lambda i, j: (i, 0)),
            pl.BlockSpec((tm, RW_WIDTH), lambda i, j: (i, 0)),
            pl.BlockSpec((tm, RW_WIDTH), lambda i, j: (i, 0)),
            pl.BlockSpec((tm, TN), lambda i, j: (i, gp0 + j)),
            pl.BlockSpec((tm, TN), lambda i, j: (i, gr0 + j)),
            _wspec(POOL_WIDTH, TN, layer, lambda i, j: j),
            _wspec(RW_WIDTH, TN, layer, lambda i, j: j),
        ],
        out_specs=pl.BlockSpec((tm, TN), lambda i, j: (i, j)),
        scratch_shapes=[pltpu.VMEM((tm, RW_WIDTH), BF16)],
        compiler_params=_params(("parallel", "arbitrary")),
        name="merge_branches",
    )(ya, yb, g, proj, proj, wp, wr)


def _final_norm_kernel(x_ref, g_ref, o_ref):
    o_ref[...] = _rms(x_ref[...], g_ref[...])


def final_norm(x, g, *, row0, rows):
    k = x.shape[1]
    tm = 512
    rb0 = row0 // tm
    return pl.pallas_call(
        _final_norm_kernel,
        out_shape=jax.ShapeDtypeStruct((rows, k), F32),
        grid=(rows // tm,),
        in_specs=[pl.BlockSpec((tm, k), lambda i: (rb0 + i, 0)), pl.BlockSpec((1, k), lambda i: (0, 0))],
        out_specs=pl.BlockSpec((tm, k), lambda i: (i, 0)),
        compiler_params=_params(("parallel",)),
        name="final_norm",
    )(x, g.reshape(1, k))


def _seq_tile(t_len):
    return min(t_len, 256)


def _pool_kernel(*refs, tt, pos0, has_dst):
    if has_dst:
        u_ref, past_ref, pw_ref, ps_ref, _, o_ref, np_ref, carry = refs
    else:
        u_ref, past_ref, pw_ref, ps_ref, o_ref, np_ref, carry = refs
    t = pl.program_id(1)

    @pl.when(t == 0)
    def _():
        carry[...] = past_ref[0]

    u = u_ref[...]
    ext = jnp.concatenate([carry[...], u], axis=0)
    carry[...] = ext[tt:tt + POOL_PAD]
    np_ref[0] = ext[tt:tt + POOL_PAD]
    pos = pos0 + t * tt + lax.broadcasted_iota(jnp.int32, (tt, 1), 0)
    outs = []
    for gi, win in enumerate(POOL_WINDOWS):
        sl = slice(gi * POOL_GC, (gi + 1) * POOL_GC)
        s = ext[:, sl]
        sh = 1
        while sh < win:
            s = s + pltpu.roll(s, sh, axis=0)
            sh *= 2
        cnt = jnp.minimum(pos + 1, win).astype(F32)
        d = s[POOL_PAD:] / cnt - u[:, sl]
        outs.append(jnp.dot(d.astype(BF16), pw_ref[gi], preferred_element_type=F32))
    o_ref[...] = (jnp.concatenate(outs, axis=-1) * ps_ref[...]).astype(BF16)


def pool_mixer(proj, past, pool_w, pool_scale, dst, *, row0, batch, t_len, pos0):
    tt = _seq_tile(t_len)
    nt = t_len // tt
    rb0 = row0 // tt
    cb = COL_POOL // POOL_WIDTH
    has_dst = dst is not None
    in_specs = [
        pl.BlockSpec((tt, POOL_WIDTH), lambda b, t: (rb0 + b * nt + t, cb)),
        pl.BlockSpec((1, POOL_PAD, POOL_WIDTH), lambda b, t: (b, 0, 0)),
        pl.BlockSpec((POOL_GROUPS, POOL_GC, POOL_GC), lambda b, t: (0, 0, 0)),
        pl.BlockSpec((1, POOL_WIDTH), lambda b, t: (0, 0)),
    ]
    args = [proj, past, pool_w, pool_scale.reshape(1, POOL_WIDTH)]
    if has_dst:
        in_specs.append(pl.BlockSpec(memory_space=pl.ANY))
        args.append(dst)
    return pl.pallas_call(
        functools.partial(_pool_kernel, tt=tt, pos0=pos0, has_dst=has_dst),
        out_shape=(jax.ShapeDtypeStruct((proj.shape[0], POOL_WIDTH), BF16),
                   jax.ShapeDtypeStruct((batch, POOL_PAD, POOL_WIDTH), F32)),
        grid=(batch, nt),
        in_specs=in_specs,
        out_specs=(pl.BlockSpec((tt, POOL_WIDTH), lambda b, t: (rb0 + b * nt + t, 0)),
                   pl.BlockSpec((1, POOL_PAD, POOL_WIDTH), lambda b, t: (b, 0, 0))),
        scratch_shapes=[pltpu.VMEM((POOL_PAD, POOL_WIDTH), F32)],
        input_output_aliases={4: 0} if has_dst else {},
        compiler_params=_params(("parallel", "arbitrary")),
        name="pool_mixer",
    )(*args)


def _shift_mix(cur, carry_row, mu):
    rolled = pltpu.roll(cur, 1, axis=0)
    row = lax.broadcasted_iota(jnp.int32, cur.shape, 0)
    prev = jnp.where(row == 0, carry_row, rolled)
    return cur + (prev - cur) * mu


def _softplus(x):
    return jnp.maximum(x, 0.0) + jnp.log1p(jnp.exp(-jnp.abs(x)))


def _rw_prep_kernel(*refs, tt, has_dst):
    (r_ref, k_ref, v_ref, l_ref, sp_ref, spl_ref, mu_ref, mul_ref,
     w0_ref, a0_ref, kk_ref, ka_ref, wd_ref, wa_ref, wg_ref, seg_ref) = refs[:16]
    pre_ref, g_ref, ns_ref, nsl_ref, carry, carry_l = refs[17 if has_dst else 16:]
    t = pl.program_id(1)

    @pl.when(t == 0)
    def _():
        carry[0:3, :] = sp_ref[0]
        carry_l[0:1, :] = spl_ref[0]

    r = r_ref[...]
    k = k_ref[...]
    v = v_ref[...]
    lo = l_ref[...]
    xr = _shift_mix(r, carry[0:1, :], mu_ref[0:1, :])
    xk = _shift_mix(k, carry[1:2, :], mu_ref[1:2, :])
    xv = _shift_mix(v, carry[2:3, :], mu_ref[2:3, :])
    xl = _shift_mix(lo, carry_l[0:1, :], mul_ref[...])
    last = jnp.concatenate([r[tt - 1:tt, :], k[tt - 1:tt, :], v[tt - 1:tt, :]], axis=0)
    carry[0:3, :] = last
    carry_l[0:1, :] = lo[tt - 1:tt, :]
    ns_ref[0] = last
    nsl_ref[0] = lo[tt - 1:tt, :]

    xda = xl[:, 0:LANES]
    xg = xl[:, LANES:LANES + LORA_GATE_PAD]
    zd = jnp.dot(jnp.tanh(xda).astype(BF16), wd_ref[...], preferred_element_type=F32)
    w = -_softplus(-(w0_ref[...] + zd)) - 0.5
    decay = jnp.exp(-jnp.exp(w))
    za = jnp.dot(xda.astype(BF16), wa_ref[...], preferred_element_type=F32)
    a = jax.nn.sigmoid(a0_ref[...] + za)
    g_ref[...] = jnp.dot(jax.nn.sigmoid(xg).astype(BF16), wg_ref[...], preferred_element_type=F32)

    kkr = xk * kk_ref[...]
    sq = kkr * kkr
    sq_hi = sq.astype(BF16)
    sq_lo = (sq - sq_hi.astype(F32)).astype(BF16)
    seg = seg_ref[...]
    parts = []
    for c in range(RW_WIDTH // LANES):
        sl = slice(c * LANES, (c + 1) * LANES)
        parts.append(jnp.dot(sq_hi[:, sl], seg, preferred_element_type=F32)
                     + jnp.dot(sq_lo[:, sl], seg, preferred_element_type=F32))
    ss = jnp.concatenate(parts, axis=-1)
    kk = kkr / jnp.maximum(jnp.sqrt(ss), 1e-12)

    pre_ref[0] = xr
    pre_ref[1] = decay
    pre_ref[2] = xk * (1.0 + (a - 1.0) * ka_ref[...])
    pre_ref[3] = xv
    pre_ref[4] = kk
    pre_ref[5] = kk * a


def rw_prep(proj, sp_rkv, sp_lora, lw, dst, *, row0, batch, t_len):
    tt = _seq_tile(t_len)
    nt = t_len // tt
    rb0 = row0 // tt
    lb = COL_LORA // LORA_PAD
    m = batch * t_len
    has_dst = dst is not None
    row = lambda c: pl.BlockSpec((tt, RW_WIDTH), lambda b, t: (rb0 + b * nt + t, c))
    full = lambda shape: pl.BlockSpec(shape, lambda b, t: tuple(0 for _ in shape))
    in_specs = [
        row(0), row(1), row(2),
        pl.BlockSpec((tt, LORA_PAD), lambda b, t: (rb0 + b * nt + t, lb)),
        pl.BlockSpec((1, 3, RW_WIDTH), lambda b, t: (b, 0, 0)),
        pl.BlockSpec((1, 1, LORA_PAD), lambda b, t: (b, 0, 0)),
        full((3, RW_WIDTH)), full((1, LORA_PAD)),
        full((1, RW_WIDTH)), full((1, RW_WIDTH)), full((1, RW_WIDTH)), full((1, RW_WIDTH)),
        full((LANES, RW_WIDTH)), full((LANES, RW_WIDTH)), full((LORA_GATE_PAD, RW_WIDTH)),
        full((LANES, LANES)),
    ]
    args = [proj, proj, proj, proj, sp_rkv, sp_lora, lw["mu_rkv"], lw["mu_lora"],
            lw["w0"], lw["a0"], lw["k_k"], lw["k_a"], lw["wd"], lw["wa"], lw["wg"], lw["seg"]]
    if has_dst:
        in_specs.append(pl.BlockSpec(memory_space=pl.ANY))
        args.append(dst)
    return pl.pallas_call(
        functools.partial(_rw_prep_kernel, tt=tt, has_dst=has_dst),
        out_shape=(jax.ShapeDtypeStruct((6, m, RW_WIDTH), F32),
                   jax.ShapeDtypeStruct((proj.shape[0], RW_WIDTH), F32),
                   jax.ShapeDtypeStruct((batch, 3, RW_WIDTH), F32),
                   jax.ShapeDtypeStruct((batch, 1, LORA_PAD), F32)),
        grid=(batch, nt),
        in_specs=in_specs,
        out_specs=(pl.BlockSpec((6, tt, RW_WIDTH), lambda b, t: (0, b * nt + t, 0)),
                   pl.BlockSpec((tt, RW_WIDTH), lambda b, t: (rb0 + b * nt + t, 0)),
                   pl.BlockSpec((1, 3, RW_WIDTH), lambda b, t: (b, 0, 0)),
                   pl.BlockSpec((1, 1, LORA_PAD), lambda b, t: (b, 0, 0))),
        scratch_shapes=[pltpu.VMEM((8, RW_WIDTH), F32), pltpu.VMEM((8, LORA_PAD), F32)],
        input_output_aliases={16: 1} if has_dst else {},
        compiler_params=_params(("parallel", "arbitrary")),
        name="rw_prep",
    )(*args)


def _wkv_kernel(xs_ref, s0_ref, gain_ref, bias_ref, rk_ref, y_ref, s_ref, *, tc):
    @pl.when(pl.program_id(0) == 0)
    def _():
        s_ref[...] = s0_ref[...]

    def step(t, carry):
        acc = [jnp.zeros((RW_HEAD, LANES), F32) for _ in range(2)]
        for k in range(RW_HEAD):
            acc[k % 2] = acc[k % 2] + s_ref[k] * xs_ref[t, 4, k:k + 1, :]
        sa = -(acc[0] + acc[1])
        vv = xs_ref[t, 3]
        yacc = [jnp.zeros((RW_HEAD, LANES), F32) for _ in range(2)]
        for k in range(RW_HEAD):
            sn = (s_ref[k] * xs_ref[t, 1, k:k + 1, :] + sa * xs_ref[t, 5, k:k + 1, :]
                  + vv * xs_ref[t, 2, k:k + 1, :])
            s_ref[k] = sn
            yacc[k % 2] = yacc[k % 2] + sn * xs_ref[t, 0, k:k + 1, :]
        y = yacc[0] + yacc[1]
        mean = jnp.mean(y, axis=0, keepdims=True)
        yc = y - mean
        var = jnp.mean(yc * yc, axis=0, keepdims=True)
        yn = yc * lax.rsqrt(var + GN_EPS) * gain_ref[...] + bias_ref[...]
        bonus = jnp.sum(xs_ref[t, 0] * xs_ref[t, 2] * rk_ref[...], axis=0, keepdims=True)
        y_ref[t] = yn + bonus * vv
        return carry

    lax.fori_loop(0, tc, step, 0)


def wkv_scan(xs, s0, gain_t, bias_t, rk_t):
    t_len = xs.shape[0]
    tc = 16
    tab = pl.BlockSpec((RW_HEAD, LANES), lambda i: (0, 0))
    st = pl.BlockSpec((RW_HEAD, RW_HEAD, LANES), lambda i: (0, 0, 0))
    return pl.pallas_call(
        functools.partial(_wkv_kernel, tc=tc),
        out_shape=(jax.ShapeDtypeStruct((t_len, RW_HEAD, LANES), F32),
                   jax.ShapeDtypeStruct((RW_HEAD, RW_HEAD, LANES), F32)),
        grid=(t_len // tc,),
        in_specs=[pl.BlockSpec((tc, 6, RW_HEAD, LANES), lambda i: (i, 0, 0, 0)), st, tab, tab, tab],
        out_specs=(pl.BlockSpec((tc, RW_HEAD, LANES), lambda i: (i, 0, 0)), st),
        compiler_params=_params(("arbitrary",)),
        name="wkv_scan",
    )(xs, s0, gain_t, bias_t, rk_t)


def _head_table(vec, batch):
    tab = vec.reshape(RW_HEADS, RW_HEAD).T
    return jnp.tile(tab, (1, batch))


def _to_scan_layout(pre, batch, t_len):
    x = pre.reshape(6, batch, t_len, RW_HEADS, RW_HEAD)
    return x.transpose(2, 0, 4, 1, 3).reshape(t_len, 6, RW_HEAD, batch * RW_HEADS)


def _from_scan_layout(y, batch, t_len):
    x = y.reshape(t_len, RW_HEAD, batch, RW_HEADS)
    return x.transpose(2, 0, 3, 1).reshape(batch * t_len, RW_WIDTH)


def _stacked_weights(w_in, pool_w, w_decay_up, w_aaa_up, w_gate_up, proj_pool, proj_rwkv, w_out,
                     w_ffn_gate, w_ffn_up, w_ffn_down, w_ple_gate, w_ple_proj):
    c_rw = POOL_WIDTH
    c_lora = POOL_WIDTH + 3 * RW_WIDTH
    c_gp = POOL_WIDTH + RW_PROJ
    c_gr = c_gp + D_MODEL
    w_in_r = jnp.concatenate([
        w_in[:, :, c_rw:c_lora], w_in[:, :, :POOL_WIDTH], w_in[:, :, c_gp:c_gr], w_in[:, :, c_gr:],
        w_in[:, :, c_lora:c_gp], jnp.zeros((DEPTH, D_MODEL, LORA_PAD - LORA), F32)], axis=2).astype(BF16)
    zpad = lambda rows: jnp.zeros((DEPTH, rows, RW_WIDTH), F32)
    return dict(
        w_in=w_in_r, pool_w=pool_w.astype(BF16),
        wd=jnp.concatenate([w_decay_up, zpad(LANES - DECAY_LORA)], axis=1).astype(BF16),
        wa=jnp.concatenate([zpad(DECAY_LORA), w_aaa_up], axis=1).astype(BF16),
        wg=jnp.concatenate([w_gate_up, zpad(LORA_GATE_PAD - GATE_LORA)], axis=1).astype(BF16),
        proj_pool=proj_pool.astype(BF16), proj_rwkv=proj_rwkv.astype(BF16), w_out=w_out.astype(BF16),
        w_ffn_gate=w_ffn_gate.astype(BF16), w_ffn_up=w_ffn_up.astype(BF16), w_ffn_down=w_ffn_down.astype(BF16),
        w_ple_gate=w_ple_gate.astype(BF16), w_ple_proj=w_ple_proj.astype(BF16))


def _mixers(proj, lw, shift_prev, pool_past, s0, ya_dst, g_dst, *, row0, batch, t_len, pos0):
    past = jnp.pad(pool_past, ((0, 0), (POOL_PAD - POOL_PAST, 0), (0, 0)))
    ya, pool_rows = pool_mixer(proj, past, lw["pool_w"], lw["pool_scale"], ya_dst,
                               row0=row0, batch=batch, t_len=t_len, pos0=pos0)
    sp_rkv = shift_prev[:, :3 * RW_WIDTH].reshape(batch, 3, RW_WIDTH)
    sp_lora = jnp.pad(shift_prev[:, 3 * RW_WIDTH:], ((0, 0), (0, LORA_PAD - LORA))).reshape(batch, 1, LORA_PAD)
    pre, g, ns_rkv, ns_lora = rw_prep(proj, sp_rkv, sp_lora, lw, g_dst, row0=row0, batch=batch, t_len=t_len)
    xs = _to_scan_layout(pre, batch, t_len)
    s0_t = s0.transpose(3, 2, 0, 1).reshape(RW_HEAD, RW_HEAD, batch * RW_HEADS)
    y_t, s_t = wkv_scan(xs, s0_t, _head_table(lw["gn_gain"], batch), _head_table(lw["gn_bias"], batch),
                        _head_table(lw["r_k"], batch))
    yb = _from_scan_layout(y_t, batch, t_len)
    s_new = s_t.reshape(RW_HEAD, RW_HEAD, batch, RW_HEADS).transpose(2, 3, 1, 0)
    new_shift = jnp.concatenate([ns_rkv.reshape(batch, 3 * RW_WIDTH), ns_lora[:, 0, :LORA]], axis=1)
    new_pool = pool_rows[:, POOL_PAD - POOL_PAST:]
    return ya, yb, g, new_shift, new_pool, s_new


def kernel(x_prompt, x_sample, state_shift, state_pool, state_wkv, p_prompt, p_sample, norm_mix, w_in, mu_shift, pool_w, pool_scale, w0, w_decay_up, a0, w_aaa_up, w_gate_up, k_k, k_a, r_k, gn_gain, gn_bias, proj_pool, proj_rwkv, w_out, norm_ffn, w_ffn_gate, w_ffn_up, w_ffn_down, norm_ple, w_ple_gate, w_ple_proj, norm_final):
    bp, tp, _ = x_prompt.shape
    bs, ts, _ = x_sample.shape
    mp = bp * tp
    ms = bs * ts
    x = jnp.concatenate([x_prompt.reshape(mp, D_MODEL), x_sample.reshape(ms, D_MODEL)], axis=0)
    sw = _stacked_weights(w_in, pool_w, w_decay_up, w_aaa_up, w_gate_up, proj_pool, proj_rwkv, w_out,
                          w_ffn_gate, w_ffn_up, w_ffn_down, w_ple_gate, w_ple_proj)
    lane = jnp.arange(LANES) // RW_HEAD
    seg = (lane[:, None] == lane[None, :]).astype(BF16)
    zeros_shift = jnp.zeros((bp, RW_PROJ), F32)
    zeros_pool = jnp.zeros((bp, POOL_PAST, POOL_WIDTH), F32)
    zeros_wkv = jnp.zeros((bp, RW_HEADS, RW_HEAD, RW_HEAD), F32)
    row = lambda v: v.reshape(1, -1)
    outs_p, outs_s = [], []
    for i in range(DEPTH):
        mu = mu_shift[i]
        lw = dict(
            mu_rkv=mu[:3 * RW_WIDTH].reshape(3, RW_WIDTH),
            mu_lora=jnp.pad(mu[3 * RW_WIDTH:], (0, LORA_PAD - LORA)).reshape(1, LORA_PAD),
            pool_w=sw["pool_w"][i], pool_scale=pool_scale[i],
            w0=row(w0[i]), a0=row(a0[i]), k_k=row(k_k[i]), k_a=row(k_a[i]),
            wd=sw["wd"][i], wa=sw["wa"][i], wg=sw["wg"][i], seg=seg,
            r_k=r_k[i].reshape(RW_WIDTH), gn_gain=gn_gain[i], gn_bias=gn_bias[i])
        proj = norm_matmul(x, norm_mix[i], sw["w_in"], i)
        ya, yb_p, g, sh_p, po_p, wk_p = _mixers(
            proj, lw, zeros_shift, zeros_pool, zeros_wkv, None, None, row0=0, batch=bp, t_len=tp, pos0=0)
        ya, yb_s, g, sh_s, po_s, wk_s = _mixers(
            proj, lw, state_shift[i], state_pool[i], state_wkv[i], ya, g,
            row0=mp, batch=bs, t_len=ts, pos0=PAST_LEN)
        outs_p.append((sh_p, po_p, wk_p))
        outs_s.append((sh_s, po_s, wk_s))
        yb = jnp.concatenate([yb_p, yb_s], axis=0)
        merged = merge_branches(ya, yb, g, proj, sw["proj_pool"], sw["proj_rwkv"], i)
        x = out_proj(merged, sw["w_out"], x, i)
        x = ffn(x, norm_ffn[i], sw["w_ffn_gate"], sw["w_ffn_up"], sw["w_ffn_down"], i)
        p = jnp.concatenate([p_prompt[i].reshape(mp, PLE_DIM), p_sample[i].reshape(ms, PLE_DIM)], axis=0)
        x = ple_update(x, norm_ple[i], sw["w_ple_gate"], p, sw["w_ple_proj"], i)
    y_prompt = final_norm(x, norm_final, row0=0, rows=mp).reshape(bp, tp, D_MODEL)
    y_sample = final_norm(x, norm_final, row0=mp, rows=ms).reshape(bs, ts, D_MODEL)
    stack = lambda outs, j: jnp.stack([o[j] for o in outs])
    return (y_prompt, y_sample,
            stack(outs_p, 0), stack(outs_p, 1), stack(outs_p, 2),
            stack(outs_s, 0), stack(outs_s, 1), stack(outs_s, 2))
```

```python
import functools

import jax
import jax.numpy as jnp
from jax import lax
from jax.experimental import pallas as pl
from jax.experimental.pallas import tpu as pltpu

F32 = jnp.float32
BF16 = jnp.bfloat16

D_MODEL = 2048
DEPTH = 4
PAST_LEN = 1024
PLE_DIM = 256
POOL_WINDOWS = (2, 4, 8, 16)
POOL_GROUPS = 4
POOL_WIDTH = D_MODEL // 2
POOL_GC = POOL_WIDTH // POOL_GROUPS
POOL_PAST = max(POOL_WINDOWS) - 1
RW_WIDTH = D_MODEL // 2
RW_HEAD = 64
RW_HEADS = RW_WIDTH // RW_HEAD
DECAY_LORA = 64
AAA_LORA = 64
GATE_LORA = 160
LORA = DECAY_LORA + AAA_LORA + GATE_LORA
RW_PROJ = 3 * RW_WIDTH + LORA
D_FF = 5632
NORM_EPS = 1e-6
GN_EPS = 64e-5

LANES = 128
POOL_PAD = 16
LORA_PAD = 512
LORA_GATE_PAD = 256
PAIRS = RW_WIDTH // LANES
CHUNK = 64
SUB = 16
COL_RKV = 0
COL_POOL = 3 * RW_WIDTH
COL_GPOOL = COL_POOL + POOL_WIDTH
COL_GRW = COL_GPOOL + D_MODEL
COL_LORA = COL_GRW + D_MODEL
IN_COLS_PAD = COL_LORA + LORA_PAD

TN = 512
TM_WIDE = 1536
TM_ROW = 768
VMEM_LIMIT = 56 * 1024 * 1024


def _params(sem):
    return pltpu.CompilerParams(dimension_semantics=sem, vmem_limit_bytes=VMEM_LIMIT)


def _rms(x, g):
    ms = jnp.mean(x * x, axis=-1, keepdims=True)
    return x * lax.rsqrt(ms + NORM_EPS) * g


def _wspec(k, n, layer, col_of):
    return pl.BlockSpec((None, k, n), lambda *idx: (layer, 0, col_of(*idx)))


def _norm_mm_kernel(x_ref, g_ref, w_ref, o_ref, h_scr):
    @pl.when(pl.program_id(1) == 0)
    def _():
        h_scr[...] = _rms(x_ref[...], g_ref[...]).astype(BF16)

    o_ref[...] = jnp.dot(h_scr[...], w_ref[...], preferred_element_type=F32)


def norm_matmul(x, g, w, layer):
    m, k = x.shape
    n = w.shape[2]
    tm = TM_WIDE
    return pl.pallas_call(
        _norm_mm_kernel,
        out_shape=jax.ShapeDtypeStruct((m, n), F32),
        grid=(m // tm, n // TN),
        in_specs=[
            pl.BlockSpec((tm, k), lambda i, j: (i, 0)),
            pl.BlockSpec((1, k), lambda i, j: (0, 0)),
            _wspec(k, TN, layer, lambda i, j: j),
        ],
        out_specs=pl.BlockSpec((tm, TN), lambda i, j: (i, j)),
        scratch_shapes=[pltpu.VMEM((tm, k), BF16)],
        compiler_params=_params(("parallel", "arbitrary")),
        name="norm_matmul",
    )(x, g.reshape(1, k), w)


def _out_proj_kernel(a_ref, w_ref, r_ref, o_ref):
    n = o_ref.shape[1]
    for c in range(n // TN):
        sl = slice(c * TN, (c + 1) * TN)
        o_ref[:, sl] = r_ref[:, sl] + jnp.dot(a_ref[...], w_ref[:, sl], preferred_element_type=F32)


def out_proj(a, w, res, layer):
    m, k = a.shape
    n = w.shape[2]
    tm = TM_ROW
    return pl.pallas_call(
        _out_proj_kernel,
        out_shape=jax.ShapeDtypeStruct((m, n), F32),
        grid=(m // tm,),
        in_specs=[
            pl.BlockSpec((tm, k), lambda i: (i, 0)),
            _wspec(k, n, layer, lambda i: 0),
            pl.BlockSpec((tm, n), lambda i: (i, 0)),
        ],
        out_specs=pl.BlockSpec((tm, n), lambda i: (i, 0)),
        compiler_params=_params(("parallel",)),
        name="out_proj",
    )(a, w, res)


def _ffn_kernel(x_ref, g_ref, wg_ref, wu_ref, wd_ref, o_ref, h_scr):
    @pl.when(pl.program_id(1) == 0)
    def _():
        x = x_ref[...]
        h_scr[...] = _rms(x, g_ref[...]).astype(BF16)
        o_ref[...] = x

    h = h_scr[...]
    gate = jnp.dot(h, wg_ref[...], preferred_element_type=F32)
    up = jnp.dot(h, wu_ref[...], preferred_element_type=F32)
    act = (gate * jax.nn.sigmoid(gate) * up).astype(BF16)
    o_ref[...] += jnp.dot(act, wd_ref[...], preferred_element_type=F32)


def ffn(x, g, wg, wu, wd, layer):
    m, k = x.shape
    f = wg.shape[2]
    tm = TM_ROW
    return pl.pallas_call(
        _ffn_kernel,
        out_shape=jax.ShapeDtypeStruct((m, k), F32),
        grid=(m // tm, f // TN),
        in_specs=[
            pl.BlockSpec((tm, k), lambda i, j: (i, 0)),
            pl.BlockSpec((1, k), lambda i, j: (0, 0)),
            _wspec(k, TN, layer, lambda i, j: j),
            _wspec(k, TN, layer, lambda i, j: j),
            pl.BlockSpec((None, TN, k), lambda i, j: (layer, j, 0)),
        ],
        out_specs=pl.BlockSpec((tm, k), lambda i, j: (i, 0)),
        scratch_shapes=[pltpu.VMEM((tm, k), BF16)],
        compiler_params=_params(("parallel", "arbitrary")),
        name="ffn",
    )(x, g.reshape(1, k), wg, wu, wd)


def _ple_kernel(x_ref, g_ref, wg_ref, p_ref, wp_ref, o_ref, h_scr):
    n = o_ref.shape[1]
    h_scr[...] = _rms(x_ref[...], g_ref[...]).astype(BF16)
    pb = p_ref[...].astype(BF16)
    for c in range(n // TN):
        sl = slice(c * TN, (c + 1) * TN)
        gate = jnp.dot(h_scr[...], wg_ref[:, sl], preferred_element_type=F32)
        emb = jnp.dot(pb, wp_ref[:, sl], preferred_element_type=F32)
        o_ref[:, sl] = x_ref[:, sl] + jax.nn.sigmoid(gate) * emb


def ple_update(x, g, wg, p, wp, layer):
    m, k = x.shape
    n = wg.shape[2]
    kp = p.shape[1]
    tm = TM_ROW
    return pl.pallas_call(
        _ple_kernel,
        out_shape=jax.ShapeDtypeStruct((m, n), F32),
        grid=(m // tm,),
        in_specs=[
            pl.BlockSpec((tm, k), lambda i: (i, 0)),
            pl.BlockSpec((1, k), lambda i: (0, 0)),
            _wspec(k, n, layer, lambda i: 0),
            pl.BlockSpec((tm, kp), lambda i: (i, 0)),
            _wspec(kp, n, layer, lambda i: 0),
        ],
        out_specs=pl.BlockSpec((tm, n), lambda i: (i, 0)),
        scratch_shapes=[pltpu.VMEM((tm, k), BF16)],
        compiler_params=_params(("parallel",)),
        name="ple_update",
    )(x, g.reshape(1, k), wg, p, wp)


def _merge_kernel(ya_ref, yb_ref, g_ref, gp_ref, gr_ref, wp_ref, wr_ref, o_ref, ybg_scr):
    @pl.when(pl.program_id(1) == 0)
    def _():
        ybg_scr[...] = (yb_ref[...] * g_ref[...]).astype(BF16)

    a = jnp.dot(ya_ref[...], wp_ref[...], preferred_element_type=F32)
    b = jnp.dot(ybg_scr[...], wr_ref[...], preferred_element_type=F32)
    o_ref[...] = (jax.nn.sigmoid(gp_ref[...]) * a + jax.nn.sigmoid(gr_ref[...]) * b).astype(BF16)


def merge_branches(ya, yb, g, proj, wp, wr, layer):
    m = ya.shape[0]
    n = wp.shape[2]
    tm = TM_ROW
    gp0 = COL_GPOOL // TN
    gr0 = COL_GRW // TN
    return pl.pallas_call(
        _merge_kernel,
        out_shape=jax.ShapeDtypeStruct((m, n), BF16),
        grid=(m // tm, n // TN),
        in_specs=[
            pl.BlockSpec((tm, POOL_WIDTH), lambda i, j: (i, 0)),
            pl.BlockSpec((tm, RW_WIDTH), lambda i, j: (i, 0)),
            pl.BlockSpec((tm, RW_WIDTH), lambda i, j: (i, 0)),
            pl.BlockSpec((tm, TN), lambda i, j: (i, gp0 + j)),
            pl.BlockSpec((tm, TN), lambda i, j: (i, gr0 + j)),
            _wspec(POOL_WIDTH, TN, layer, lambda i, j: j),
            _wspec(RW_WIDTH, TN, layer, lambda i, j: j),
        ],
        out_specs=pl.BlockSpec((tm, TN), lambda i, j: (i, j)),
        scratch_shapes=[pltpu.VMEM((tm, RW_WIDTH), BF16)],
        compiler_params=_params(("parallel", "arbitrary")),
        name="merge_branches",
    )(ya, yb, g, proj, proj, wp, wr)


def _final_norm_kernel(x_ref, g_ref, o_ref):
    o_ref[...] = _rms(x_ref[...], g_ref[...])


def final_norm(x, g, *, row0, rows):
    k = x.shape[1]
    tm = 512
    rb0 = row0 // tm
    return pl.pallas_call(
        _final_norm_kernel,
        out_shape=jax.ShapeDtypeStruct((rows, k), F32),
        grid=(rows // tm,),
        in_specs=[pl.BlockSpec((tm, k), lambda i: (rb0 + i, 0)), pl.BlockSpec((1, k), lambda i: (0, 0))],
        out_specs=pl.BlockSpec((tm, k), lambda i: (i, 0)),
        compiler_params=_params(("parallel",)),
        name="final_norm",
    )(x, g.reshape(1, k))


def _seq_tile(t_len):
    return min(t_len, 256)


def _pool_kernel(*refs, tt, pos0, has_dst):
    if has_dst:
        u_ref, past_ref, pw_ref, ps_ref, _, o_ref, np_ref, carry = refs
    else:
        u_ref, past_ref, pw_ref, ps_ref, o_ref, np_ref, carry = refs
    t = pl.program_id(1)

    @pl.when(t == 0)
    def _():
        carry[...] = past_ref[0]

    u = u_ref[...]
    ext = jnp.concatenate([carry[...], u], axis=0)
    carry[...] = ext[tt:tt + POOL_PAD]
    np_ref[0] = ext[tt:tt + POOL_PAD]
    pos = pos0 + t * tt + lax.broadcasted_iota(jnp.int32, (tt, 1), 0)
    outs = []
    for gi, win in enumerate(POOL_WINDOWS):
        sl = slice(gi * POOL_GC, (gi + 1) * POOL_GC)
        s = ext[:, sl]
        sh = 1
        while sh < win:
            s = s + pltpu.roll(s, sh, axis=0)
            sh *= 2
        cnt = jnp.minimum(pos + 1, win).astype(F32)
        d = s[POOL_PAD:] / cnt - u[:, sl]
        outs.append(jnp.dot(d.astype(BF16), pw_ref[gi], preferred_element_type=F32))
    o_ref[...] = (jnp.concatenate(outs, axis=-1) * ps_ref[...]).astype(BF16)


def pool_mixer(proj, past, pool_w, pool_scale, dst, *, row0, batch, t_len, pos0):
    tt = _seq_tile(t_len)
    nt = t_len // tt
    rb0 = row0 // tt
    cb = COL_POOL // POOL_WIDTH
    has_dst = dst is not None
    in_specs = [
        pl.BlockSpec((tt, POOL_WIDTH), lambda b, t: (rb0 + b * nt + t, cb)),
        pl.BlockSpec((1, POOL_PAD, POOL_WIDTH), lambda b, t: (b, 0, 0)),
        pl.BlockSpec((POOL_GROUPS, POOL_GC, POOL_GC), lambda b, t: (0, 0, 0)),
        pl.BlockSpec((1, POOL_WIDTH), lambda b, t: (0, 0)),
    ]
    args = [proj, past, pool_w, pool_scale.reshape(1, POOL_WIDTH)]
    if has_dst:
        in_specs.append(pl.BlockSpec(memory_space=pl.ANY))
        args.append(dst)
    return pl.pallas_call(
        functools.partial(_pool_kernel, tt=tt, pos0=pos0, has_dst=has_dst),
        out_shape=(jax.ShapeDtypeStruct((proj.shape[0], POOL_WIDTH), BF16),
                   jax.ShapeDtypeStruct((batch, POOL_PAD, POOL_WIDTH), F32)),
        grid=(batch, nt),
        in_specs=in_specs,
        out_specs=(pl.BlockSpec((tt, POOL_WIDTH), lambda b, t: (rb0 + b * nt + t, 0)),
                   pl.BlockSpec((1, POOL_PAD, POOL_WIDTH), lambda b, t: (b, 0, 0))),
        scratch_shapes=[pltpu.VMEM((POOL_PAD, POOL_WIDTH), F32)],
        input_output_aliases={4: 0} if has_dst else {},
        compiler_params=_params(("parallel", "arbitrary")),
        name="pool_mixer",
    )(*args)


def _shift_mix(cur, carry_row, mu):
    rolled = pltpu.roll(cur, 1, axis=0)
    row = lax.broadcasted_iota(jnp.int32, cur.shape, 0)
    prev = jnp.where(row == 0, carry_row, rolled)
    return cur + (prev - cur) * mu


def _softplus(x):
    return jnp.maximum(x, 0.0) + jnp.log1p(jnp.exp(-jnp.abs(x)))


def _rw_prep_kernel(*refs, tt, has_dst):
    (r_ref, k_ref, v_ref, l_ref, sp_ref, spl_ref, mu_ref, mul_ref,
     w0_ref, a0_ref, kk_ref, ka_ref, wd_ref, wa_ref, wg_ref, seg_ref) = refs[:16]
    pre_ref, g_ref, ns_ref, nsl_ref, carry, carry_l = refs[17 if has_dst else 16:]
    t = pl.program_id(1)

    @pl.when(t == 0)
    def _():
        carry[0:3, :] = sp_ref[0]
        carry_l[0:1, :] = spl_ref[0]

    r = r_ref[...]
    k = k_ref[...]
    v = v_ref[...]
    lo = l_ref[...]
    xr = _shift_mix(r, carry[0:1, :], mu_ref[0:1, :])
    xk = _shift_mix(k, carry[1:2, :], mu_ref[1:2, :])
    xv = _shift_mix(v, carry[2:3, :], mu_ref[2:3, :])
    xl = _shift_mix(lo, carry_l[0:1, :], mul_ref[...])
    last = jnp.concatenate([r[tt - 1:tt, :], k[tt - 1:tt, :], v[tt - 1:tt, :]], axis=0)
    carry[0:3, :] = last
    carry_l[0:1, :] = lo[tt - 1:tt, :]
    ns_ref[0] = last
    nsl_ref[0] = lo[tt - 1:tt, :]

    xda = xl[:, 0:LANES]
    xg = xl[:, LANES:LANES + LORA_GATE_PAD]
    zd = jnp.dot(jnp.tanh(xda).astype(BF16), wd_ref[...], preferred_element_type=F32)
    w = -_softplus(-(w0_ref[...] + zd)) - 0.5
    log_decay = -jnp.exp(w)
    za =jnp.dot(xda.astype(BF16), wa_ref[...], preferred_element_type=F32)
    a = jax.nn.sigmoid(a0_ref[...] + za)
    g_ref[...] = jnp.dot(jax.nn.sigmoid(xg).astype(BF16), wg_ref[...], preferred_element_type=F32)

    kkr = xk * kk_ref[...]
    sq = kkr * kkr
    sq_hi = sq.astype(BF16)
    sq_lo = (sq - sq_hi.astype(F32)).astype(BF16)
    seg = seg_ref[...]
    parts = []
    for c in range(RW_WIDTH // LANES):
        sl = slice(c * LANES, (c + 1) * LANES)
        parts.append(jnp.dot(sq_hi[:, sl], seg, preferred_element_type=F32)
                     + jnp.dot(sq_lo[:, sl], seg, preferred_element_type=F32))
    ss = jnp.concatenate(parts, axis=-1)
    kk = kkr / jnp.maximum(jnp.sqrt(ss), 1e-12)

    pre_ref[0] = xr
    pre_ref[1] = log_decay
    pre_ref[2] = xk * (1.0 + (a - 1.0) * ka_ref[...])
    pre_ref[3] = xv
    pre_ref[4] = kk
    pre_ref[5] = kk * a


def rw_prep(proj, sp_rkv, sp_lora, lw, dst, *, row0, batch, t_len):
    tt = _seq_tile(t_len)
    nt = t_len // tt
    rb0 = row0 // tt
    lb = COL_LORA // LORA_PAD
    m = batch * t_len
    has_dst = dst is not None
    row = lambda c: pl.BlockSpec((tt, RW_WIDTH), lambda b, t: (rb0 + b * nt + t, c))
    full = lambda shape: pl.BlockSpec(shape, lambda b, t: tuple(0 for _ in shape))
    in_specs = [
        row(0), row(1), row(2),
        pl.BlockSpec((tt, LORA_PAD), lambda b, t: (rb0 + b * nt + t, lb)),
        pl.BlockSpec((1, 3, RW_WIDTH), lambda b, t: (b, 0, 0)),
        pl.BlockSpec((1, 1, LORA_PAD), lambda b, t: (b, 0, 0)),
        full((3, RW_WIDTH)), full((1, LORA_PAD)),
        full((1, RW_WIDTH)), full((1, RW_WIDTH)), full((1, RW_WIDTH)), full((1, RW_WIDTH)),
        full((LANES, RW_WIDTH)), full((LANES, RW_WIDTH)), full((LORA_GATE_PAD, RW_WIDTH)),
        full((LANES, LANES)),
    ]
    args = [proj, proj, proj, proj, sp_rkv, sp_lora, lw["mu_rkv"], lw["mu_lora"],
            lw["w0"], lw["a0"], lw["k_k"], lw["k_a"], lw["wd"], lw["wa"], lw["wg"], lw["seg"]]
    if has_dst:
        in_specs.append(pl.BlockSpec(memory_space=pl.ANY))
        args.append(dst)
    return pl.pallas_call(
        functools.partial(_rw_prep_kernel, tt=tt, has_dst=has_dst),
        out_shape=(jax.ShapeDtypeStruct((6, m, RW_WIDTH), F32),
                   jax.ShapeDtypeStruct((proj.shape[0], RW_WIDTH), F32),
                   jax.ShapeDtypeStruct((batch, 3, RW_WIDTH), F32),
                   jax.ShapeDtypeStruct((batch, 1, LORA_PAD), F32)),
        grid=(batch, nt),
        in_specs=in_specs,
        out_specs=(pl.BlockSpec((6, tt, RW_WIDTH), lambda b, t: (0, b * nt + t, 0)),
                   pl.BlockSpec((tt, RW_WIDTH), lambda b, t: (rb0 + b * nt + t, 0)),
                   pl.BlockSpec((1, 3, RW_WIDTH), lambda b, t: (b, 0, 0)),
                   pl.BlockSpec((1, 1, LORA_PAD), lambda b, t: (b, 0, 0))),
        scratch_shapes=[pltpu.VMEM((8, RW_WIDTH), F32), pltpu.VMEM((8, LORA_PAD), F32)],
        input_output_aliases={16: 1} if has_dst else {},
        compiler_params=_params(("parallel", "arbitrary")),
        name="rw_prep",
    )(*args)


def _mm(a, b):
    return jnp.dot(a.astype(BF16), b.astype(BF16), preferred_element_type=F32)


def _mm_nt(a, b):
    return lax.dot_general(a.astype(BF16), b.astype(BF16), (((1,), (1,)), ((), ())),
                           preferred_element_type=F32)


def _mm_tn(a, b):
    return lax.dot_general(a.astype(BF16), b.astype(BF16), (((0,), (0,)), ((), ())),
                           preferred_element_type=F32)


def _split2(x):
    hi = x.astype(BF16)
    lo = (x - hi.astype(F32)).astype(BF16)
    return hi, lo


def _mm_hilo(a, b):
    ah, al = _split2(a)
    bh, bl = _split2(b)
    d = lambda x, y: jnp.dot(x, y, preferred_element_type=F32)
    return d(ah, bh) + d(ah, bl) + d(al, bh)


def _segsum(x, seg):
    hi, lo = _split2(x)
    parts = []
    for c in range(PAIRS):
        sl = slice(c * LANES, (c + 1) * LANES)
        parts.append(jnp.dot(hi[:, sl], seg, preferred_element_type=F32)
                     + jnp.dot(lo[:, sl], seg, preferred_element_type=F32))
    return jnp.concatenate(parts, axis=-1)


def _stack2(x, m0):
    return jnp.concatenate([jnp.where(m0, x, 0.0), jnp.where(m0, 0.0, x)], axis=0)


def _wkv_chunk_kernel(*refs, has_dst):
    pre_ref, s0_ref, gain_ref, bias_ref, rk_ref, seg_ref, tri_ref = refs[:7]
    y_ref, s_ref = refs[8 if has_dst else 7:]
    c = CHUNK

    @pl.when(pl.program_id(1) == 0)
    def _():
        s_ref[...] = s0_ref[...]

    r = pre_ref[0]
    logw = pre_ref[1]
    k = pre_ref[2]
    v = pre_ref[3]
    kk = pre_ref[4]
    bb = pre_ref[5]

    hi = logw.astype(BF16)
    r1 = logw - hi.astype(F32)
    mid = r1.astype(BF16)
    lo = (r1 - mid.astype(F32)).astype(BF16)
    tri = tri_ref[...]
    linc = (jnp.dot(tri, hi, preferred_element_type=F32) + jnp.dot(tri, mid, preferred_element_type=F32)
            + jnp.dot(tri, lo, preferred_element_type=F32))
    lexc = linc - logw
    ltot = linc[c - 1:c, :]
    e_ninc = jnp.exp(-linc)
    kt = kk * jnp.exp(lexc)
    rt = r * jnp.exp(linc)
    bh = bb * e_ninc
    kh = k * e_ninc
    e_rem = jnp.exp(ltot - linc)
    bbar = bb * e_rem
    kbar = k * e_rem
    gtot = jnp.exp(ltot)

    ri = lax.broadcasted_iota(jnp.int32, (2 * c, 2 * c), 0)
    ci = lax.broadcasted_iota(jnp.int32, (2 * c, 2 * c), 1)
    strict = (ci % c) < (ri % c)
    incl = (ci % c) <= (ri % c)
    blk = (ri // SUB) == (ci // SUB)
    eye = ri == ci
    m0 = lax.broadcasted_iota(jnp.int32, (c, LANES), 1) < RW_HEAD

    pairs = range(PAIRS)
    lanes = [slice(p * LANES, (p + 1) * LANES) for p in pairs]
    each = lambda fn, *cols: [fn(*args) for args in zip(*cols)]
    stacked = lambda x: [_stack2(x[:, sl], m0) for sl in lanes]
    kt2, rt2, bh2, kh2, bbar2, kbar2, v2 = (stacked(x) for x in (kt, rt, bh, kh, bbar, kbar, v))

    aa = each(lambda a, b_, c_, d: _mm_nt(jnp.concatenate([a, b_], axis=0), jnp.concatenate([c_, d], axis=0)),
              kt2, rt2, bh2, kh2)
    lp = [jnp.where(strict, x[0:2 * c, 0:2 * c], 0.0) for x in aa]
    akp = [jnp.where(strict, x[0:2 * c, 2 * c:4 * c], 0.0) for x in aa]
    arb = [jnp.where(incl, x[2 * c:4 * c, 0:2 * c], 0.0) for x in aa]
    ark = [jnp.where(incl, x[2 * c:4 * c, 2 * c:4 * c], 0.0) for x in aa]

    ld = [jnp.where(blk, x, 0.0) for x in lp]
    lo_ = each(lambda x, y_: x - y_, lp, ld)
    p2 = each(_mm, ld, ld)
    p4 = each(_mm, p2, p2)
    p8 = each(_mm, p4, p4)
    e1 = each(lambda a, b_, m: a - b_ - m, p2, ld, each(_mm, ld, p2))
    e2 = each(lambda a, b_, m: a + b_ + m, e1, p4, each(_mm, e1, p4))
    dm = each(lambda a, b_, m: a + b_ + m, e2, p8, each(_mm, e2, p8))
    n1 = each(lambda a, m: a + m, lo_, each(_mm, dm, lo_))
    n2 = each(_mm, n1, n1)
    f = each(lambda a, b_, m: a - b_ - m, n2, n1, each(_mm, n1, n2))
    tm = each(lambda a, b_, m: a + b_ + m, f, dm, each(_mm, f, dm))

    akv = each(_mm, akp, v2)
    u = each(lambda a, b_: jnp.concatenate([a, b_], axis=1), kt2, akv)
    pq = each(lambda a, m: a + m, u, each(_mm, tm, u))
    rbpq = each(_mm, arb, pq)
    ry = each(lambda a, m: a - m[:, 0:LANES], rt2, rbpq)
    y0 = each(lambda m, n_: m - n_[:, LANES:2 * LANES], each(_mm, ark, v2), rbpq)
    btpq = each(_mm_tn, bbar2, pq)
    g2 = [jnp.where(eye, gtot[:, sl], 0.0) - m[:, 0:LANES] for sl, m in zip(lanes, btpq)]
    h2 = each(lambda m, n_: m - n_[:, LANES:2 * LANES], each(_mm_tn, kbar2, v2), btpq)

    s = [s_ref[0, p] for p in pairs]
    y2 = each(lambda a, b_, o: _mm_hilo(a, b_) + o, ry, s, y0)
    s_new = each(lambda a, b_, o: _mm_hilo(a, b_) + o, g2, s, h2)
    for p in pairs:
        s_ref[0, p] = s_new[p]
    ys = [x[0:c] + x[c:2 * c] for x in y2]

    y = jnp.concatenate(ys, axis=-1)
    seg = seg_ref[...]
    inv_n = 1.0 / RW_HEAD
    mean = _segsum(y, seg) * inv_n
    yc = y - mean
    var = _segsum(yc * yc, seg) * inv_n
    yn = yc * lax.rsqrt(var + GN_EPS) * gain_ref[...] + bias_ref[...]
    bonus = _segsum(r * k * rk_ref[...], seg)
    y_ref[...] = yn + bonus * v


def wkv_chunked(pre, s0_blk, gain, bias, rk, seg, dst, *, rows_total, row0, batch, t_len):
    c = CHUNK
    nt = t_len // c
    rb0 = row0 // c
    tri = (jnp.arange(c)[:, None] >= jnp.arange(c)[None, :]).astype(BF16)
    has_dst = dst is not None
    vec = pl.BlockSpec((1, RW_WIDTH), lambda b, t: (0, 0))
    st = pl.BlockSpec((1, PAIRS, LANES, LANES), lambda b, t: (b, 0, 0, 0))
    in_specs = [pl.BlockSpec((6, c, RW_WIDTH), lambda b, t: (0, b * nt + t, 0)), st, vec, vec, vec,
                pl.BlockSpec((LANES, LANES), lambda b, t: (0, 0)),
                pl.BlockSpec((c, c), lambda b, t: (0, 0))]
    args = [pre, s0_blk, gain.reshape(1, -1), bias.reshape(1, -1), rk.reshape(1, -1), seg, tri]
    if has_dst:
        in_specs.append(pl.BlockSpec(memory_space=pl.ANY))
        args.append(dst)
    return pl.pallas_call(
        functools.partial(_wkv_chunk_kernel, has_dst=has_dst),
        out_shape=(jax.ShapeDtypeStruct((rows_total, RW_WIDTH), F32),
                   jax.ShapeDtypeStruct((batch, PAIRS, LANES, LANES), F32)),
        grid=(batch, nt),
        in_specs=in_specs,
        out_specs=(pl.BlockSpec((c, RW_WIDTH), lambda b, t: (rb0 + b * nt + t, 0)), st),
        input_output_aliases={7: 0} if has_dst else {},
        compiler_params=_params(("parallel", "arbitrary")),
        name="wkv_chunked",
    )(*args)


def _state_to_blocks(s0):
    b = s0.shape[0]
    st = jnp.swapaxes(s0, -1, -2).reshape(b, PAIRS, 2, RW_HEAD, RW_HEAD)
    return jnp.einsum('bphkv,hg->bphkgv', st, jnp.eye(2, dtype=s0.dtype)).reshape(b, PAIRS, LANES, LANES)


def _blocks_to_state(sb):
    b = sb.shape[0]
    x = sb.reshape(b, PAIRS, 2, RW_HEAD, 2, RW_HEAD)
    d = jnp.stack([x[:, :, 0, :, 0, :], x[:, :, 1, :, 1, :]], axis=2)
    return jnp.swapaxes(d.reshape(b, RW_HEADS, RW_HEAD, RW_HEAD), -1, -2)


def _stacked_weights(w_in, pool_w, w_decay_up, w_aaa_up, w_gate_up, proj_pool, proj_rwkv, w_out,
                     w_ffn_gate, w_ffn_up, w_ffn_down, w_ple_gate, w_ple_proj):
    c_rw = POOL_WIDTH
    c_lora = POOL_WIDTH + 3 * RW_WIDTH
    c_gp = POOL_WIDTH + RW_PROJ
    c_gr = c_gp + D_MODEL
    w_in_r = jnp.concatenate([
        w_in[:, :, c_rw:c_lora], w_in[:, :, :POOL_WIDTH], w_in[:, :, c_gp:c_gr], w_in[:, :, c_gr:],
        w_in[:, :, c_lora:c_gp], jnp.zeros((DEPTH, D_MODEL, LORA_PAD - LORA), F32)], axis=2).astype(BF16)
    zpad = lambda rows: jnp.zeros((DEPTH, rows, RW_WIDTH), F32)
    return dict(
        w_in=w_in_r, pool_w=pool_w.astype(BF16),
        wd=jnp.concatenate([w_decay_up, zpad(LANES - DECAY_LORA)], axis=1).astype(BF16),
        wa=jnp.concatenate([zpad(DECAY_LORA), w_aaa_up], axis=1).astype(BF16),
        wg=jnp.concatenate([w_gate_up, zpad(LORA_GATE_PAD - GATE_LORA)], axis=1).astype(BF16),
        proj_pool=proj_pool.astype(BF16), proj_rwkv=proj_rwkv.astype(BF16), w_out=w_out.astype(BF16),
        w_ffn_gate=w_ffn_gate.astype(BF16), w_ffn_up=w_ffn_up.astype(BF16), w_ffn_down=w_ffn_down.astype(BF16),
        w_ple_gate=w_ple_gate.astype(BF16), w_ple_proj=w_ple_proj.astype(BF16))


def _mixers(proj, lw, shift_prev, pool_past, s0, ya_dst, g_dst, yb_dst, *, row0, batch, t_len, pos0):
    past = jnp.pad(pool_past, ((0, 0), (POOL_PAD - POOL_PAST, 0), (0, 0)))
    ya, pool_rows = pool_mixer(proj, past, lw["pool_w"], lw["pool_scale"], ya_dst,
                               row0=row0, batch=batch, t_len=t_len, pos0=pos0)
    sp_rkv = shift_prev[:, :3 * RW_WIDTH].reshape(batch, 3, RW_WIDTH)
    sp_lora = jnp.pad(shift_prev[:, 3 * RW_WIDTH:], ((0, 0), (0, LORA_PAD - LORA))).reshape(batch, 1, LORA_PAD)
    pre, g, ns_rkv, ns_lora = rw_prep(proj, sp_rkv, sp_lora, lw, g_dst, row0=row0, batch=batch, t_len=t_len)
    yb, s_blk = wkv_chunked(pre, _state_to_blocks(s0), lw["gn_gain"], lw["gn_bias"], lw["r_k"], lw["seg"],
                            yb_dst, rows_total=proj.shape[0], row0=row0, batch=batch, t_len=t_len)
    new_shift = jnp.concatenate([ns_rkv.reshape(batch, 3 * RW_WIDTH), ns_lora[:, 0, :LORA]], axis=1)
    new_pool = pool_rows[:, POOL_PAD - POOL_PAST:]
    return ya, yb, g, new_shift, new_pool, _blocks_to_state(s_blk)


def kernel(x_prompt, x_sample, state_shift, state_pool, state_wkv, p_prompt, p_sample, norm_mix, w_in, mu_shift, pool_w, pool_scale, w0, w_decay_up, a0, w_aaa_up, w_gate_up, k_k, k_a, r_k, gn_gain, gn_bias, proj_pool, proj_rwkv, w_out, norm_ffn, w_ffn_gate, w_ffn_up, w_ffn_down, norm_ple, w_ple_gate, w_ple_proj, norm_final):
    bp, tp, _ = x_prompt.shape
    bs, ts, _ = x_sample.shape
    mp = bp * tp
    ms = bs * ts
    x = jnp.concatenate([x_prompt.reshape(mp, D_MODEL), x_sample.reshape(ms, D_MODEL)], axis=0)
    sw = _stacked_weights(w_in, pool_w, w_decay_up, w_aaa_up, w_gate_up, proj_pool, proj_rwkv, w_out,
                          w_ffn_gate, w_ffn_up, w_ffn_down, w_ple_gate, w_ple_proj)
    lane = jnp.arange(LANES) // RW_HEAD
    seg = (lane[:, None] == lane[None, :]).astype(BF16)
    zeros_shift = jnp.zeros((bp, RW_PROJ), F32)
    zeros_pool = jnp.zeros((bp, POOL_PAST, POOL_WIDTH), F32)
    zeros_wkv = jnp.zeros((bp, RW_HEADS, RW_HEAD, RW_HEAD), F32)
    row = lambda v: v.reshape(1, -1)
    outs_p, outs_s = [], []
    for i in range(DEPTH):
        mu = mu_shift[i]
        lw = dict(
            mu_rkv=mu[:3 * RW_WIDTH].reshape(3, RW_WIDTH),
            mu_lora=jnp.pad(mu[3 * RW_WIDTH:], (0, LORA_PAD - LORA)).reshape(1, LORA_PAD),
            pool_w=sw["pool_w"][i], pool_scale=pool_scale[i],
            w0=row(w0[i]), a0=row(a0[i]), k_k=row(k_k[i]), k_a=row(k_a[i]),
            wd=sw["wd"][i], wa=sw["wa"][i], wg=sw["wg"][i], seg=seg,
            r_k=r_k[i].reshape(RW_WIDTH), gn_gain=gn_gain[i], gn_bias=gn_bias[i])
        proj = norm_matmul(x, norm_mix[i], sw["w_in"], i)
        ya, yb, g, sh_p, po_p, wk_p = _mixers(
            proj, lw, zeros_shift, zeros_pool, zeros_wkv, None, None, None, row0=0, batch=bp, t_len=tp, pos0=0)
        ya, yb, g, sh_s, po_s, wk_s = _mixers(
            proj, lw, state_shift[i], state_pool[i], state_wkv[i], ya, g, yb,
            row0=mp, batch=bs, t_len=ts, pos0=PAST_LEN)
        outs_p.append((sh_p, po_p, wk_p))
        outs_s.append((sh_s, po_s, wk_s))
        merged = merge_branches(ya, yb, g, proj, sw["proj_pool"], sw["proj_rwkv"], i)
        x = out_proj(merged, sw["w_out"], x, i)
        x = ffn(x, norm_ffn[i], sw["w_ffn_gate"], sw["w_ffn_up"], sw["w_ffn_down"], i)
        p = jnp.concatenate([p_prompt[i].reshape(mp, PLE_DIM), p_sample[i].reshape(ms, PLE_DIM)], axis=0)
        x = ple_update(x, norm_ple[i], sw["w_ple_gate"], p, sw["w_ple_proj"], i)
    y_prompt = final_norm(x, norm_final, row0=0, rows=mp).reshape(bp, tp, D_MODEL)
    y_sample = final_norm(x, norm_final, row0=mp, rows=ms).reshape(bs, ts, D_MODEL)
    stack = lambda outs, j: jnp.stack([o[j] for o in outs])
    return (y_prompt, y_sample,
            stack(outs_p, 0), stack(outs_p, 1), stack(outs_p, 2),
            stack(outs_s, 0), stack(outs_s, 1), stack(outs_s, 2))
```

```python
import functools

import jax
import jax.numpy as jnp
from jax import lax
from jax.experimental import pallas as pl
from jax.experimental.pallas import tpu as pltpu

F32 = jnp.float32
BF16 = jnp.bfloat16

D_MODEL = 2048
DEPTH = 4
PAST_LEN = 1024
PLE_DIM = 256
POOL_WINDOWS = (2, 4, 8, 16)
POOL_GROUPS = 4
POOL_WIDTH = D_MODEL // 2
POOL_GC = POOL_WIDTH // POOL_GROUPS
POOL_PAST = max(POOL_WINDOWS) - 1
RW_WIDTH = D_MODEL // 2
RW_HEAD = 64
RW_HEADS = RW_WIDTH // RW_HEAD
DECAY_LORA = 64
AAA_LORA = 64
GATE_LORA = 160
LORA = DECAY_LORA + AAA_LORA + GATE_LORA
RW_PROJ = 3 * RW_WIDTH + LORA
D_FF = 5632
NORM_EPS = 1e-6
GN_EPS = 64e-5

LANES = 128
POOL_PAD = 16
LORA_PAD = 512
LORA_GATE_PAD = 256
PAIRS = RW_WIDTH // LANES
CHUNK = 64
SUB = 16
COL_RKV = 0
COL_POOL = 3 * RW_WIDTH
COL_GPOOL = COL_POOL + POOL_WIDTH
COL_GRW = COL_GPOOL + D_MODEL
COL_LORA = COL_GRW + D_MODEL
IN_COLS_PAD = COL_LORA + LORA_PAD

TN = 512
TM_WIDE = 1536
TM_ROW = 768
VMEM_LIMIT = 56 * 1024 * 1024


def _params(sem):
    return pltpu.CompilerParams(dimension_semantics=sem, vmem_limit_bytes=VMEM_LIMIT)


def _rms(x, g):
    ms = jnp.mean(x * x, axis=-1, keepdims=True)
    return x * lax.rsqrt(ms + NORM_EPS) * g


def _wspec(k, n, layer, col_of):
    return pl.BlockSpec((None, k, n), lambda *idx: (layer, 0, col_of(*idx)))


def _norm_mm_kernel(x_ref, g_ref, w_ref, o_ref, h_scr):
    @pl.when(pl.program_id(1) == 0)
    def _():
        h_scr[...] = _rms(x_ref[...], g_ref[...]).astype(BF16)

    o_ref[...] = jnp.dot(h_scr[...], w_ref[...], preferred_element_type=F32)


def norm_matmul(x, g, w, layer):
    m, k = x.shape
    n = w.shape[2]
    tm = TM_WIDE
    return pl.pallas_call(
        _norm_mm_kernel,
        out_shape=jax.ShapeDtypeStruct((m, n), F32),
        grid=(m // tm, n // TN),
        in_specs=[
            pl.BlockSpec((tm, k), lambda i, j: (i, 0)),
            pl.BlockSpec((1, k), lambda i, j: (0, 0)),
            _wspec(k, TN, layer, lambda i, j: j),
        ],
        out_specs=pl.BlockSpec((tm, TN), lambda i, j: (i, j)),
        scratch_shapes=[pltpu.VMEM((tm, k), BF16)],
        compiler_params=_params(("parallel", "arbitrary")),
        name="norm_matmul",
    )(x, g.reshape(1, k), w)


def _out_proj_kernel(a_ref, w_ref, r_ref, o_ref):
    n = o_ref.shape[1]
    for c in range(n // TN):
        sl = slice(c * TN, (c + 1) * TN)
        o_ref[:, sl] = r_ref[:, sl] + jnp.dot(a_ref[...], w_ref[:, sl], preferred_element_type=F32)


def out_proj(a, w, res, layer):
    m, k = a.shape
    n = w.shape[2]
    tm = TM_ROW
    return pl.pallas_call(
        _out_proj_kernel,
        out_shape=jax.ShapeDtypeStruct((m, n), F32),
        grid=(m // tm,),
        in_specs=[
            pl.BlockSpec((tm, k), lambda i: (i, 0)),
            _wspec(k, n, layer, lambda i: 0),
            pl.BlockSpec((tm, n), lambda i: (i, 0)),
        ],
        out_specs=pl.BlockSpec((tm, n), lambda i: (i, 0)),
        compiler_params=_params(("parallel",)),
        name="out_proj",
    )(a, w, res)


def _ffn_kernel(x_ref, g_ref, wg_ref, wu_ref, wd_ref, o_ref, h_scr):
    @pl.when(pl.program_id(1) == 0)
    def _():
        x = x_ref[...]
        h_scr[...] = _rms(x, g_ref[...]).astype(BF16)
        o_ref[...] = x

    h = h_scr[...]
    gate = jnp.dot(h, wg_ref[...], preferred_element_type=F32)
    up = jnp.dot(h, wu_ref[...], preferred_element_type=F32)
    act = (gate * jax.nn.sigmoid(gate) * up).astype(BF16)
    o_ref[...] += jnp.dot(act, wd_ref[...], preferred_element_type=F32)


def ffn(x, g, wg, wu, wd, layer):
    m, k = x.shape
    f = wg.shape[2]
    tm = TM_ROW
    return pl.pallas_call(
        _ffn_kernel,
        out_shape=jax.ShapeDtypeStruct((m, k), F32),
        grid=(m // tm, f // TN),
        in_specs=[
            pl.BlockSpec((tm, k), lambda i, j: (i, 0)),
            pl.BlockSpec((1, k), lambda i, j: (0, 0)),
            _wspec(k, TN, layer, lambda i, j: j),
            _wspec(k, TN, layer, lambda i, j: j),
            pl.BlockSpec((None, TN, k), lambda i, j: (layer, j, 0)),
        ],
        out_specs=pl.BlockSpec((tm, k), lambda i, j: (i, 0)),
        scratch_shapes=[pltpu.VMEM((tm, k), BF16)],
        compiler_params=_params(("parallel", "arbitrary")),
        name="ffn",
    )(x, g.reshape(1, k), wg, wu, wd)


def _ple_kernel(x_ref, g_ref, wg_ref, p_ref, wp_ref, o_ref, h_scr):
    n = o_ref.shape[1]
    h_scr[...] = _rms(x_ref[...], g_ref[...]).astype(BF16)
    pb = p_ref[...].astype(BF16)
    for c in range(n // TN):
        sl = slice(c * TN, (c + 1) * TN)
        gate = jnp.dot(h_scr[...], wg_ref[:, sl], preferred_element_type=F32)
        emb = jnp.dot(pb, wp_ref[:, sl], preferred_element_type=F32)
        o_ref[:, sl] = x_ref[:, sl] + jax.nn.sigmoid(gate) * emb


def ple_update(x, g, wg, p, wp, layer):
    m, k = x.shape
    n = wg.shape[2]
    kp = p.shape[1]
    tm = TM_ROW
    return pl.pallas_call(
        _ple_kernel,
        out_shape=jax.ShapeDtypeStruct((m, n), F32),
        grid=(m // tm,),
        in_specs=[
            pl.BlockSpec((tm, k), lambda i: (i, 0)),
            pl.BlockSpec((1, k), lambda i: (0, 0)),
            _wspec(k, n, layer, lambda i: 0),
            pl.BlockSpec((tm, kp), lambda i: (i, 0)),
            _wspec(kp, n, layer, lambda i: 0),
        ],
        out_specs=pl.BlockSpec((tm, n), lambda i: (i, 0)),
        scratch_shapes=[pltpu.VMEM((tm, k), BF16)],
        compiler_params=_params(("parallel",)),
        name="ple_update",
    )(x, g.reshape(1, k), wg, p, wp)


def _merge_kernel(ya_ref, yb_ref, g_ref, gp_ref, gr_ref, wp_ref, wr_ref, o_ref, ybg_scr):
    @pl.when(pl.program_id(1) == 0)
    def _():
        ybg_scr[...] = (yb_ref[...] * g_ref[...]).astype(BF16)

    a = jnp.dot(ya_ref[...], wp_ref[...], preferred_element_type=F32)
    b = jnp.dot(ybg_scr[...], wr_ref[...], preferred_element_type=F32)
    o_ref[...] = (jax.nn.sigmoid(gp_ref[...]) * a + jax.nn.sigmoid(gr_ref[...]) * b).astype(BF16)


def merge_branches(ya, yb, g, proj, wp, wr, layer):
    m = ya.shape[0]
    n = wp.shape[2]
    tm = TM_ROW
    gp0 = COL_GPOOL // TN
    gr0 = COL_GRW // TN
    return pl.pallas_call(
        _merge_kernel,
        out_shape=jax.ShapeDtypeStruct((m, n), BF16),
        grid=(m // tm, n // TN),
        in_specs=[
            pl.BlockSpec((tm, POOL_WIDTH), lambda i, j: (i, 0)),
            pl.BlockSpec((tm, RW_WIDTH), lambda i, j: (i, 0)),
            pl.BlockSpec((tm, RW_WIDTH), lambda i, j: (i, 0)),
            pl.BlockSpec((tm, TN), lambda i, j: (i, gp0 + j)),
            pl.BlockSpec((tm, TN), lambda i, j: (i, gr0 + j)),
            _wspec(POOL_WIDTH, TN, layer, lambda i, j: j),
            _wspec(RW_WIDTH, TN, layer, lambda i, j: j),
        ],
        out_specs=pl.BlockSpec((tm, TN), lambda i, j: (i, j)),
        scratch_shapes=[pltpu.VMEM((tm, RW_WIDTH), BF16)],
        compiler_params=_params(("parallel", "arbitrary")),
        name="merge_branches",
    )(ya, yb, g, proj, proj, wp, wr)


def _final_norm_kernel(x_ref, g_ref, o_ref):
    o_ref[...] = _rms(x_ref[...], g_ref[...])


def final_norm(x, g, *, row0, rows):
    k = x.shape[1]
    tm = 512
    rb0 = row0 // tm
    return pl.pallas_call(
        _final_norm_kernel,
        out_shape=jax.ShapeDtypeStruct((rows, k), F32),
        grid=(rows // tm,),
        in_specs=[pl.BlockSpec((tm, k), lambda i: (rb0 + i, 0)), pl.BlockSpec((1, k), lambda i: (0, 0))],
        out_specs=pl.BlockSpec((tm, k), lambda i: (i, 0)),
        compiler_params=_params(("parallel",)),
        name="final_norm",
    )(x, g.reshape(1, k))


def _shift_mix(cur, carry_row, mu):
    rolled = pltpu.roll(cur, 1, axis=0)
    row = lax.broadcasted_iota(jnp.int32, cur.shape, 0)
    prev = jnp.where(row == 0, carry_row, rolled)
    return cur + (prev - cur) * mu


def _softplus(x):
    return jnp.maximum(x, 0.0) + jnp.log1p(jnp.exp(-jnp.abs(x)))


def _mm(a, b):
    return jnp.dot(a.astype(BF16), b.astype(BF16), preferred_element_type=F32)


def _mm_nt(a, b):
    return lax.dot_general(a.astype(BF16), b.astype(BF16), (((1,), (1,)), ((), ())),
                           preferred_element_type=F32)


def _mm_tn(a, b):
    return lax.dot_general(a.astype(BF16), b.astype(BF16), (((0,), (0,)), ((), ())),
                           preferred_element_type=F32)


def _split2(x):
    hi = x.astype(BF16)
    lo = (x - hi.astype(F32)).astype(BF16)
    return hi, lo


def _mm_hilo(a, b):
    ah, al = _split2(a)
    bh, bl = _split2(b)
    d = lambda x, y: jnp.dot(x, y, preferred_element_type=F32)
    return d(ah, bh) + d(ah, bl) + d(al, bh)


def _segsum(x, seg):
    hi, lo = _split2(x)
    parts = []
    for c in range(PAIRS):
        sl = slice(c * LANES, (c + 1) * LANES)
        parts.append(jnp.dot(hi[:, sl], seg, preferred_element_type=F32)
                     + jnp.dot(lo[:, sl], seg, preferred_element_type=F32))
    return jnp.concatenate(parts, axis=-1)


def _stack2(x, m0):
    return jnp.concatenate([jnp.where(m0, x, 0.0), jnp.where(m0, 0.0, x)], axis=0)


def _seq_kernel(r_ref, k_ref, v_ref, l_ref, u_ref, sp_ref, spl_ref, past_ref, s0_ref,
                mu_ref, mul_ref, w0_ref, a0_ref, kkw_ref, ka_ref, wd_ref, wa_ref, wg_ref,
                pw_ref, ps_ref, gain_ref, bias_ref, rk_ref, seg_ref, tri_ref,
                ya_ref, yb_ref, g_ref, ns_ref, nsl_ref, np_ref, s_ref,
                carry, carry_l, carry_p, *, prompt_tiles, tiles_p, tiles_s):
    c = CHUNK
    i = pl.program_id(0)
    in_prompt = i < prompt_tiles
    t = jnp.where(in_prompt, i % tiles_p, (i - prompt_tiles) % tiles_s)
    pos0 = jnp.where(in_prompt, 0, PAST_LEN) + t * c

    @pl.when(t == 0)
    def _():
        carry[0:3, :] = sp_ref[0]
        carry_l[0:1, :] = spl_ref[0]
        carry_p[...] = past_ref[0]
        s_ref[...] = s0_ref[...]

    u = u_ref[...]
    ext = jnp.concatenate([carry_p[...], u], axis=0)
    carry_p[...] = ext[c:c + POOL_PAD]
    np_ref[0] = ext[c:c + POOL_PAD]
    pos = pos0 + lax.broadcasted_iota(jnp.int32, (c, 1), 0)
    outs = []
    for gi, win in enumerate(POOL_WINDOWS):
        sl = slice(gi * POOL_GC, (gi + 1) * POOL_GC)
        sw = ext[:, sl]
        sh = 1
        while sh < win:
            sw = sw + pltpu.roll(sw, sh, axis=0)
            sh *= 2
        cnt = jnp.minimum(pos + 1, win).astype(F32)
        dlt = sw[POOL_PAD:] / cnt - u[:, sl]
        outs.append(jnp.dot(dlt.astype(BF16), pw_ref[gi], preferred_element_type=F32))
    ya_ref[...] = (jnp.concatenate(outs, axis=-1) * ps_ref[...]).astype(BF16)

    r_in = r_ref[...]
    k_in = k_ref[...]
    v_in = v_ref[...]
    lo_in = l_ref[...]
    r = _shift_mix(r_in, carry[0:1, :], mu_ref[0:1, :])
    xk = _shift_mix(k_in, carry[1:2, :], mu_ref[1:2, :])
    v = _shift_mix(v_in, carry[2:3, :], mu_ref[2:3, :])
    xl = _shift_mix(lo_in, carry_l[0:1, :], mul_ref[...])
    last = jnp.concatenate([r_in[c - 1:c, :], k_in[c - 1:c, :], v_in[c - 1:c, :]], axis=0)
    carry[0:3, :] = last
    carry_l[0:1, :] = lo_in[c - 1:c, :]
    ns_ref[0] = last
    nsl_ref[0] = lo_in[c - 1:c, :]

    seg = seg_ref[...]
    xda = xl[:, 0:LANES]
    xg = xl[:, LANES:LANES + LORA_GATE_PAD]
    zd = jnp.dot(jnp.tanh(xda).astype(BF16), wd_ref[...], preferred_element_type=F32)
    logw = -jnp.exp(-_softplus(-(w0_ref[...] + zd)) - 0.5)
    za = jnp.dot(xda.astype(BF16), wa_ref[...], preferred_element_type=F32)
    a = jax.nn.sigmoid(a0_ref[...] + za)
    g_ref[...] = jnp.dot(jax.nn.sigmoid(xg).astype(BF16), wg_ref[...], preferred_element_type=F32)
    kkr = xk * kkw_ref[...]
    kk = kkr / jnp.maximum(jnp.sqrt(_segsum(kkr * kkr, seg)), 1e-12)
    k = xk * (1.0 + (a - 1.0) * ka_ref[...])
    bb = kk * a

    hi = logw.astype(BF16)
    r1 = logw - hi.astype(F32)
    mid = r1.astype(BF16)
    lo = (r1 - mid.astype(F32)).astype(BF16)
    tri = tri_ref[...]
    linc = (jnp.dot(tri, hi, preferred_element_type=F32) + jnp.dot(tri, mid, preferred_element_type=F32)
            + jnp.dot(tri, lo, preferred_element_type=F32))
    lexc = linc - logw
    ltot = linc[c - 1:c, :]
    e_ninc = jnp.exp(-linc)
    kt = kk * jnp.exp(lexc)
    rt = r * jnp.exp(linc)
    bh = bb * e_ninc
    kh = k * e_ninc
    e_rem = jnp.exp(ltot - linc)
    bbar = bb * e_rem
    kbar = k * e_rem
    gtot = jnp.exp(ltot)

    ri = lax.broadcasted_iota(jnp.int32, (2 * c, 2 * c), 0)
    ci = lax.broadcasted_iota(jnp.int32, (2 * c, 2 * c), 1)
    strict = (ci % c) < (ri % c)
    incl = (ci % c) <= (ri % c)
    blk = (ri // SUB) == (ci // SUB)
    eye = ri == ci
    m0 = lax.broadcasted_iota(jnp.int32, (c, LANES), 1) < RW_HEAD

    pairs = range(PAIRS)
    lanes = [slice(p * LANES, (p + 1) * LANES) for p in pairs]
    each = lambda fn, *cols: [fn(*args) for args in zip(*cols)]
    stacked = lambda x: [_stack2(x[:, sl], m0) for sl in lanes]
    kt2, rt2, bh2, kh2, bbar2, kbar2, v2 = (stacked(x) for x in (kt, rt, bh, kh, bbar, kbar, v))

    aa = each(lambda a_, b_, c_, d_: _mm_nt(jnp.concatenate([a_, b_], axis=0), jnp.concatenate([c_, d_], axis=0)),
              kt2, rt2, bh2, kh2)
    lp = [jnp.where(strict, x[0:2 * c, 0:2 * c], 0.0) for x in aa]
    akp = [jnp.where(strict, x[0:2 * c, 2 * c:4 * c], 0.0) for x in aa]
    arb = [jnp.where(incl, x[2 * c:4 * c, 0:2 * c], 0.0) for x in aa]
    ark = [jnp.where(incl, x[2 * c:4 * c, 2 * c:4 * c], 0.0) for x in aa]

    ld = [jnp.where(blk, x, 0.0) for x in lp]
    lo_ = each(lambda x, y_: x - y_, lp, ld)
    p2 = each(_mm, ld, ld)
    p4 = each(_mm, p2, p2)
    p8 = each(_mm, p4, p4)
    e1 = each(lambda a_, b_, m: a_ - b_ - m, p2, ld, each(_mm, ld, p2))
    e2 = each(lambda a_, b_, m: a_ + b_ + m, e1, p4, each(_mm, e1, p4))
    dm = each(lambda a_, b_, m: a_ + b_ + m, e2, p8, each(_mm, e2, p8))
    n1 = each(lambda a_, m: a_ + m, lo_, each(_mm, dm, lo_))
    n2 = each(_mm, n1, n1)
    f = each(lambda a_, b_, m: a_ - b_ - m, n2, n1, each(_mm, n1, n2))
    tm = each(lambda a_, b_, m: a_ + b_ + m, f, dm, each(_mm, f, dm))

    akv = each(_mm, akp, v2)
    uu = each(lambda a_, b_: jnp.concatenate([a_, b_], axis=1), kt2, akv)
    pq = each(lambda a_, m: a_ + m, uu, each(_mm, tm, uu))
    rbpq = each(_mm, arb, pq)
    ry = each(lambda a_, m: a_ - m[:, 0:LANES], rt2, rbpq)
    y0 = each(lambda m, n_: m - n_[:, LANES:2 * LANES], each(_mm, ark, v2), rbpq)
    btpq = each(_mm_tn, bbar2, pq)
    g2 = [jnp.where(eye, gtot[:, sl], 0.0) - m[:, 0:LANES] for sl, m in zip(lanes, btpq)]
    h2 = each(lambda m, n_: m - n_[:, LANES:2 * LANES], each(_mm_tn, kbar2, v2), btpq)

    st = [s_ref[0, p] for p in pairs]
    y2 = each(lambda a_, b_, o: _mm_hilo(a_, b_) + o, ry, st, y0)
    s_new = each(lambda a_, b_, o: _mm_hilo(a_, b_) + o, g2, st, h2)
    for p in pairs:
        s_ref[0, p] = s_new[p]
    y = jnp.concatenate([x[0:c] + x[c:2 * c] for x in y2], axis=-1)

    inv_n = 1.0 / RW_HEAD
    mean = _segsum(y, seg) * inv_n
    yc = y - mean
    var = _segsum(yc * yc, seg) * inv_n
    yn = yc * lax.rsqrt(var + GN_EPS) * gain_ref[...] + bias_ref[...]
    bonus = _segsum(r * k * rk_ref[...], seg)
    yb_ref[...] = yn + bonus * v


def seq_mixers(proj, shift_rkv, shift_lora, pool_past, s0_blk, lw, *, prompt_batch, prompt_len, sample_len):
    c = CHUNK
    n_seq = shift_rkv.shape[0]
    tiles_p = prompt_len // c
    tiles_s = sample_len // c
    prompt_tiles = prompt_batch * tiles_p
    m = proj.shape[0]
    n_tiles = m // c
    seq = lambda i: jnp.where(i < prompt_tiles, i // tiles_p, prompt_batch + (i - prompt_tiles) // tiles_s)
    col = lambda width, cb: pl.BlockSpec((c, width), lambda i: (i, cb))
    per_seq = lambda *shape: pl.BlockSpec((1,) + shape, lambda i: (seq(i),) + tuple(0 for _ in shape))
    full = lambda *shape: pl.BlockSpec(shape, lambda i: tuple(0 for _ in shape))
    tri = (jnp.arange(c)[:, None] >= jnp.arange(c)[None, :]).astype(BF16)
    row = lambda v: v.reshape(1, -1)
    return pl.pallas_call(
        functools.partial(_seq_kernel, prompt_tiles=prompt_tiles, tiles_p=tiles_p, tiles_s=tiles_s),
        out_shape=(jax.ShapeDtypeStruct((m, POOL_WIDTH), BF16),
                   jax.ShapeDtypeStruct((m, RW_WIDTH), F32),
                   jax.ShapeDtypeStruct((m, RW_WIDTH), F32),
                   jax.ShapeDtypeStruct((n_seq, 3, RW_WIDTH), F32),
                   jax.ShapeDtypeStruct((n_seq, 1, LORA_PAD), F32),
                   jax.ShapeDtypeStruct((n_seq, POOL_PAD, POOL_WIDTH), F32),
                   jax.ShapeDtypeStruct((n_seq, PAIRS, LANES, LANES), F32)),
        grid=(n_tiles,),
        in_specs=[
            col(RW_WIDTH, 0), col(RW_WIDTH, 1), col(RW_WIDTH, 2), col(LORA_PAD, COL_LORA // LORA_PAD),
            col(POOL_WIDTH, COL_POOL // POOL_WIDTH),
            per_seq(3, RW_WIDTH), per_seq(1, LORA_PAD), per_seq(POOL_PAD, POOL_WIDTH), per_seq(PAIRS, LANES, LANES),
            full(3, RW_WIDTH), full(1, LORA_PAD),
            full(1, RW_WIDTH), full(1, RW_WIDTH), full(1, RW_WIDTH), full(1, RW_WIDTH),
            full(LANES, RW_WIDTH), full(LANES, RW_WIDTH), full(LORA_GATE_PAD, RW_WIDTH),
            full(POOL_GROUPS, POOL_GC, POOL_GC), full(1, POOL_WIDTH),
            full(1, RW_WIDTH), full(1, RW_WIDTH), full(1, RW_WIDTH),
            full(LANES, LANES), full(c, c),
        ],
        out_specs=(col(POOL_WIDTH, 0), col(RW_WIDTH, 0), col(RW_WIDTH, 0),
                   per_seq(3, RW_WIDTH), per_seq(1, LORA_PAD), per_seq(POOL_PAD, POOL_WIDTH),
                   per_seq(PAIRS, LANES, LANES)),
        scratch_shapes=[pltpu.VMEM((8, RW_WIDTH), F32), pltpu.VMEM((8, LORA_PAD), F32),
                        pltpu.VMEM((POOL_PAD, POOL_WIDTH), F32)],
        compiler_params=_params(("arbitrary",)),
        name="seq_mixers",
    )(proj, proj, proj, proj, proj, shift_rkv, shift_lora, pool_past, s0_blk,
      lw["mu_rkv"], lw["mu_lora"], lw["w0"], lw["a0"], lw["k_k"], lw["k_a"], lw["wd"], lw["wa"], lw["wg"],
      lw["pool_w"], row(lw["pool_scale"]), row(lw["gn_gain"]), row(lw["gn_bias"]), row(lw["r_k"]), lw["seg"], tri)


def _state_to_blocks(s0):
    b = s0.shape[0]
    st = jnp.swapaxes(s0, -1, -2).reshape(b, PAIRS, 2, RW_HEAD, RW_HEAD)
    return jnp.einsum('bphkv,hg->bphkgv', st, jnp.eye(2, dtype=s0.dtype)).reshape(b, PAIRS, LANES, LANES)


def _blocks_to_state(sb):
    b = sb.shape[0]
    x = sb.reshape(b, PAIRS, 2, RW_HEAD, 2, RW_HEAD)
    d = jnp.stack([x[:, :, 0, :, 0, :], x[:, :, 1, :, 1, :]], axis=2)
    return jnp.swapaxes(d.reshape(b, RW_HEADS, RW_HEAD, RW_HEAD), -1, -2)


def _stacked_weights(w_in, pool_w, w_decay_up, w_aaa_up, w_gate_up, proj_pool, proj_rwkv, w_out,
                     w_ffn_gate, w_ffn_up, w_ffn_down, w_ple_gate, w_ple_proj):
    c_rw = POOL_WIDTH
    c_lora = POOL_WIDTH + 3 * RW_WIDTH
    c_gp = POOL_WIDTH + RW_PROJ
    c_gr = c_gp + D_MODEL
    w_in_r = jnp.concatenate([
        w_in[:, :, c_rw:c_lora], w_in[:, :, :POOL_WIDTH], w_in[:, :, c_gp:c_gr], w_in[:, :, c_gr:],
        w_in[:, :, c_lora:c_gp], jnp.zeros((DEPTH, D_MODEL, LORA_PAD - LORA), F32)], axis=2).astype(BF16)
    zpad = lambda rows: jnp.zeros((DEPTH, rows, RW_WIDTH), F32)
    return dict(
        w_in=w_in_r, pool_w=pool_w.astype(BF16),
        wd=jnp.concatenate([w_decay_up, zpad(LANES - DECAY_LORA)], axis=1).astype(BF16),
        wa=jnp.concatenate([zpad(DECAY_LORA), w_aaa_up], axis=1).astype(BF16),
        wg=jnp.concatenate([w_gate_up, zpad(LORA_GATE_PAD - GATE_LORA)], axis=1).astype(BF16),
        proj_pool=proj_pool.astype(BF16), proj_rwkv=proj_rwkv.astype(BF16), w_out=w_out.astype(BF16),
        w_ffn_gate=w_ffn_gate.astype(BF16), w_ffn_up=w_ffn_up.astype(BF16), w_ffn_down=w_ffn_down.astype(BF16),
        w_ple_gate=w_ple_gate.astype(BF16), w_ple_proj=w_ple_proj.astype(BF16))


def kernel(x_prompt, x_sample, state_shift, state_pool, state_wkv, p_prompt, p_sample, norm_mix, w_in, mu_shift, pool_w, pool_scale, w0, w_decay_up, a0, w_aaa_up, w_gate_up, k_k, k_a, r_k, gn_gain, gn_bias, proj_pool, proj_rwkv, w_out, norm_ffn, w_ffn_gate, w_ffn_up, w_ffn_down, norm_ple, w_ple_gate, w_ple_proj, norm_final):
    bp, tp, _ = x_prompt.shape
    bs, ts, _ = x_sample.shape
    mp = bp * tp
    ms = bs * ts
    x = jnp.concatenate([x_prompt.reshape(mp, D_MODEL), x_sample.reshape(ms, D_MODEL)], axis=0)
    sw = _stacked_weights(w_in, pool_w, w_decay_up, w_aaa_up, w_gate_up, proj_pool, proj_rwkv, w_out,
                          w_ffn_gate, w_ffn_up, w_ffn_down, w_ple_gate, w_ple_proj)
    lane = jnp.arange(LANES) // RW_HEAD
    seg = (lane[:, None] == lane[None, :]).astype(BF16)
    lead = lambda a: jnp.concatenate([jnp.zeros((DEPTH, bp) + a.shape[2:], a.dtype), a], axis=1)
    shift_all = lead(state_shift)
    shift_rkv = shift_all[:, :, :3 * RW_WIDTH].reshape(DEPTH, bp + bs, 3, RW_WIDTH)
    shift_lora = jnp.pad(shift_all[:, :, 3 * RW_WIDTH:], ((0, 0), (0, 0), (0, LORA_PAD - LORA)))
    shift_lora = shift_lora.reshape(DEPTH, bp + bs, 1, LORA_PAD)
    pool_all = jnp.pad(lead(state_pool), ((0, 0), (0, 0), (POOL_PAD - POOL_PAST, 0), (0, 0)))
    wkv_all = lead(state_wkv)
    row = lambda v: v.reshape(1, -1)
    new_shift, new_pool, new_wkv = [], [], []
    for i in range(DEPTH):
        mu = mu_shift[i]
        lw = dict(
            mu_rkv=mu[:3 * RW_WIDTH].reshape(3, RW_WIDTH),
            mu_lora=jnp.pad(mu[3 * RW_WIDTH:], (0, LORA_PAD - LORA)).reshape(1, LORA_PAD),
            pool_w=sw["pool_w"][i], pool_scale=pool_scale[i],
            w0=row(w0[i]), a0=row(a0[i]), k_k=row(k_k[i]), k_a=row(k_a[i]),
            wd=sw["wd"][i], wa=sw["wa"][i], wg=sw["wg"][i], seg=seg,
            r_k=r_k[i].reshape(RW_WIDTH), gn_gain=gn_gain[i], gn_bias=gn_bias[i])
        proj = norm_matmul(x, norm_mix[i], sw["w_in"], i)
        ya, yb, g, ns_rkv, ns_lora, pool_rows, s_blk = seq_mixers(
            proj, shift_rkv[i], shift_lora[i], pool_all[i], _state_to_blocks(wkv_all[i]), lw,
            prompt_batch=bp, prompt_len=tp, sample_len=ts)
        new_shift.append(jnp.concatenate([ns_rkv.reshape(bp + bs, 3 * RW_WIDTH), ns_lora[:, 0, :LORA]], axis=1))
        new_pool.append(pool_rows[:, POOL_PAD - POOL_PAST:])
        new_wkv.append(_blocks_to_state(s_blk))
        merged = merge_branches(ya, yb, g, proj, sw["proj_pool"], sw["proj_rwkv"], i)
        x = out_proj(merged, sw["w_out"], x, i)
        x = ffn(x, norm_ffn[i], sw["w_ffn_gate"], sw["w_ffn_up"], sw["w_ffn_down"], i)
        p = jnp.concatenate([p_prompt[i].reshape(mp, PLE_DIM), p_sample[i].reshape(ms, PLE_DIM)], axis=0)
        x = ple_update(x, norm_ple[i], sw["w_ple_gate"], p, sw["w_ple_proj"], i)
    y_prompt = final_norm(x, norm_final, row0=0, rows=mp).reshape(bp, tp, D_MODEL)
    y_sample = final_norm(x, norm_final, row0=mp, rows=ms).reshape(bs, ts, D_MODEL)
    new_shift, new_pool, new_wkv = jnp.stack(new_shift), jnp.stack(new_pool), jnp.stack(new_wkv)
    return (y_prompt, y_sample, new_shift[:, :bp], new_pool[:, :bp], new_wkv[:, :bp],
            new_shift[:, bp:], new_pool[:, bp:], new_wkv[:, bp:])
```

```python
import functools

import jax
import jax.numpy as jnp
from jax import lax
from jax.experimental import pallas as pl
from jax.experimental.pallas import tpu as pltpu

F32 = jnp.float32
BF16 = jnp.bfloat16

D_MODEL = 2048
DEPTH = 4
PAST_LEN = 1024
PLE_DIM = 256
POOL_WINDOWS = (2, 4, 8, 16)
POOL_GROUPS = 4
POOL_WIDTH = D_MODEL // 2
POOL_GC = POOL_WIDTH // POOL_GROUPS
POOL_PAST = max(POOL_WINDOWS) - 1
RW_WIDTH = D_MODEL // 2
RW_HEAD = 64
RW_HEADS = RW_WIDTH // RW_HEAD
DECAY_LORA = 64
AAA_LORA = 64
GATE_LORA = 160
LORA = DECAY_LORA + AAA_LORA + GATE_LORA
RW_PROJ = 3 * RW_WIDTH + LORA
D_FF = 5632
NORM_EPS = 1e-6
GN_EPS = 64e-5

LANES = 128
POOL_PAD = 16
LORA_PAD = 512
LORA_GATE_PAD = 256
PAIRS = RW_WIDTH // LANES
CHUNK = 64
SUB = 16
COL_RKV = 0
COL_POOL = 3 * RW_WIDTH
COL_LORA = COL_POOL + POOL_WIDTH
MAIN_COLS = COL_LORA + LORA_PAD
GATE_COLS = 2 * D_MODEL

TN = 512
TM_WIDE = 1536
TM_ROW = 768
TM_MERGE = 384
VMEM_LIMIT = 56 * 1024 * 1024


def _params(sem):
    return pltpu.CompilerParams(dimension_semantics=sem, vmem_limit_bytes=VMEM_LIMIT)


def _rms(x, g):
    ms = jnp.mean(x * x, axis=-1, keepdims=True)
    return x * lax.rsqrt(ms + NORM_EPS) * g


def _wspec(k, n, layer, col_of):
    return pl.BlockSpec((None, k, n), lambda *idx: (layer, 0, col_of(*idx)))


def _in_proj_kernel(x_ref, g_ref, w_ref, main_ref, gates_ref, h_scr, *, main_tiles):
    j = pl.program_id(1)

    @pl.when(j == 0)
    def _():
        h_scr[...] = _rms(x_ref[...], g_ref[...]).astype(BF16)

    acc = jnp.dot(h_scr[...], w_ref[...], preferred_element_type=F32)

    @pl.when(j < main_tiles)
    def _():
        main_ref[...] = acc

    @pl.when(j >= main_tiles)
    def _():
        gates_ref[...] = jax.nn.sigmoid(acc).astype(BF16)


def in_proj(x, g, w, layer):
    m, k = x.shape
    tm = TM_WIDE
    main_tiles = MAIN_COLS // TN
    return pl.pallas_call(
        functools.partial(_in_proj_kernel, main_tiles=main_tiles),
        out_shape=(jax.ShapeDtypeStruct((m, MAIN_COLS), F32), jax.ShapeDtypeStruct((m, GATE_COLS), BF16)),
        grid=(m // tm, (MAIN_COLS + GATE_COLS) // TN),
        in_specs=[
            pl.BlockSpec((tm, k), lambda i, j: (i, 0)),
            pl.BlockSpec((1, k), lambda i, j: (0, 0)),
            _wspec(k, TN, layer, lambda i, j: j),
        ],
        out_specs=(pl.BlockSpec((tm, TN), lambda i, j: (i, jnp.minimum(j, main_tiles - 1))),
                   pl.BlockSpec((tm, TN), lambda i, j: (i, jnp.maximum(j - main_tiles, 0)))),
        scratch_shapes=[pltpu.VMEM((tm, k), BF16)],
        compiler_params=_params(("parallel", "arbitrary")),
        name="in_proj",
    )(x, g.reshape(1, k), w)


def _merge_out_kernel(ya_ref, yb_ref, g_ref, gates_ref, x_ref, wp_ref, wr_ref, wo_ref, o_ref, ybg_scr, m_scr):
    n = o_ref.shape[1]
    ybg_scr[...] = (yb_ref[...] * g_ref[...]).astype(BF16)
    for c in range(n // TN):
        sl = slice(c * TN, (c + 1) * TN)
        sr = slice(n + c * TN, n + (c + 1) * TN)
        a = jnp.dot(ya_ref[...], wp_ref[:, sl], preferred_element_type=F32)
        b = jnp.dot(ybg_scr[...], wr_ref[:, sl], preferred_element_type=F32)
        m_scr[:, sl] = (gates_ref[:, sl].astype(F32) * a + gates_ref[:, sr].astype(F32) * b).astype(BF16)
    for c in range(n // TN):
        sl = slice(c * TN, (c + 1) * TN)
        o_ref[:, sl] = x_ref[:, sl] + jnp.dot(m_scr[...], wo_ref[:, sl], preferred_element_type=F32)


def merge_out(ya, yb, g, gates, x, wp, wr, wo, layer):
    m, n = x.shape
    tm = TM_MERGE
    resident = lambda k_: pl.BlockSpec((None, k_, n), lambda i: (layer, 0, 0), pipeline_mode=pl.Buffered(1))
    return pl.pallas_call(
        _merge_out_kernel,
        out_shape=jax.ShapeDtypeStruct((m, n), F32),
        grid=(m // tm,),
        in_specs=[
            pl.BlockSpec((tm, POOL_WIDTH), lambda i: (i, 0)),
            pl.BlockSpec((tm, RW_WIDTH), lambda i: (i, 0)),
            pl.BlockSpec((tm, RW_WIDTH), lambda i: (i, 0)),
            pl.BlockSpec((tm, GATE_COLS), lambda i: (i, 0)),
            pl.BlockSpec((tm, n), lambda i: (i, 0)),
            resident(POOL_WIDTH), resident(RW_WIDTH), resident(n),
        ],
        out_specs=pl.BlockSpec((tm, n), lambda i: (i, 0)),
        scratch_shapes=[pltpu.VMEM((tm, RW_WIDTH), BF16), pltpu.VMEM((tm, n), BF16)],
        compiler_params=_params(("parallel",)),
        name="merge_out",
    )(ya, yb, g, gates, x, wp, wr, wo)


def _ffn_kernel(x_ref, g_ref, wg_ref, wu_ref, wd_ref, o_ref, h_scr):
    @pl.when(pl.program_id(1) == 0)
    def _():
        x = x_ref[...]
        h_scr[...] = _rms(x, g_ref[...]).astype(BF16)
        o_ref[...] = x

    h = h_scr[...]
    gate = jnp.dot(h, wg_ref[...], preferred_element_type=F32)
    up = jnp.dot(h, wu_ref[...], preferred_element_type=F32)
    act = (gate * jax.nn.sigmoid(gate) * up).astype(BF16)
    o_ref[...] += jnp.dot(act, wd_ref[...], preferred_element_type=F32)


def ffn(x, g, wg, wu, wd, layer):
    m, k = x.shape
    f = wg.shape[2]
    tm = TM_ROW
    return pl.pallas_call(
        _ffn_kernel,
        out_shape=jax.ShapeDtypeStruct((m, k), F32),
        grid=(m // tm, f // TN),
        in_specs=[
            pl.BlockSpec((tm, k), lambda i, j: (i, 0)),
            pl.BlockSpec((1, k), lambda i, j: (0, 0)),
            _wspec(k, TN, layer, lambda i, j: j),
            _wspec(k, TN, layer, lambda i, j: j),
            pl.BlockSpec((None, TN, k), lambda i, j: (layer, j, 0)),
        ],
        out_specs=pl.BlockSpec((tm, k), lambda i, j: (i, 0)),
        scratch_shapes=[pltpu.VMEM((tm, k), BF16)],
        compiler_params=_params(("parallel", "arbitrary")),
        name="ffn",
    )(x, g.reshape(1, k), wg, wu, wd)


def _ple_kernel(x_ref, g_ref, wg_ref, p_ref, wp_ref, o_ref, h_scr):
    n = o_ref.shape[1]
    h_scr[...] = _rms(x_ref[...], g_ref[...]).astype(BF16)
    pb = p_ref[...].astype(BF16)
    for c in range(n // TN):
        sl = slice(c * TN, (c + 1) * TN)
        gate = jnp.dot(h_scr[...], wg_ref[:, sl], preferred_element_type=F32)
        emb = jnp.dot(pb, wp_ref[:, sl], preferred_element_type=F32)
        o_ref[:, sl] = x_ref[:, sl] + jax.nn.sigmoid(gate) * emb


def ple_update(x, g, wg, p, wp, layer):
    m, k = x.shape
    n = wg.shape[2]
    kp = p.shape[1]
    tm = TM_ROW
    return pl.pallas_call(
        _ple_kernel,
        out_shape=jax.ShapeDtypeStruct((m, n), F32),
        grid=(m // tm,),
        in_specs=[
            pl.BlockSpec((tm, k), lambda i: (i, 0)),
            pl.BlockSpec((1, k), lambda i: (0, 0)),
            _wspec(k, n, layer, lambda i: 0),
            pl.BlockSpec((tm, kp), lambda i: (i, 0)),
            _wspec(kp, n, layer, lambda i: 0),
        ],
        out_specs=pl.BlockSpec((tm, n), lambda i: (i, 0)),
        scratch_shapes=[pltpu.VMEM((tm, k), BF16)],
        compiler_params=_params(("parallel",)),
        name="ple_update",
    )(x, g.reshape(1, k), wg, p, wp)


def _final_norm_kernel(x_ref, g_ref, o_ref):
    o_ref[...] = _rms(x_ref[...], g_ref[...])


def final_norm(x, g, *, row0, rows):
    k = x.shape[1]
    tm = 512
    rb0 = row0 // tm
    return pl.pallas_call(
        _final_norm_kernel,
        out_shape=jax.ShapeDtypeStruct((rows, k), F32),
        grid=(rows // tm,),
        in_specs=[pl.BlockSpec((tm, k), lambda i: (rb0 + i, 0)), pl.BlockSpec((1, k), lambda i: (0, 0))],
        out_specs=pl.BlockSpec((tm, k), lambda i: (i, 0)),
        compiler_params=_params(("parallel",)),
        name="final_norm",
    )(x, g.reshape(1, k))


def _shift_mix(cur, carry_row, mu):
    rolled = pltpu.roll(cur, 1, axis=0)
    row = lax.broadcasted_iota(jnp.int32, cur.shape, 0)
    prev = jnp.where(row == 0, carry_row, rolled)
    return cur + (prev - cur) * mu


def _softplus(x):
    return jnp.maximum(x, 0.0) + jnp.log1p(jnp.exp(-jnp.abs(x)))


def _mm(a, b):
    return jnp.dot(a.astype(BF16), b.astype(BF16), preferred_element_type=F32)


def _mm_nt(a, b):
    return lax.dot_general(a.astype(BF16), b.astype(BF16), (((1,), (1,)), ((), ())),
                           preferred_element_type=F32)


def _mm_tn(a, b):
    return lax.dot_general(a.astype(BF16), b.astype(BF16), (((0,), (0,)), ((), ())),
                           preferred_element_type=F32)


def _split2(x):
    hi = x.astype(BF16)
    lo = (x - hi.astype(F32)).astype(BF16)
    return hi, lo


def _mm_hilo(a, b):
    ah, al = _split2(a)
    bh, bl = _split2(b)
    d = lambda x, y: jnp.dot(x, y, preferred_element_type=F32)
    return d(ah, bh) + d(ah, bl) + d(al, bh)


def _segsum(x, seg):
    xb = x.astype(BF16)
    parts = [jnp.dot(xb[:, c * LANES:(c + 1) * LANES], seg, preferred_element_type=F32) for c in range(PAIRS)]
    return jnp.concatenate(parts, axis=-1)


def _stack2(x, m0):
    return jnp.concatenate([jnp.where(m0, x, 0.0), jnp.where(m0, 0.0, x)], axis=0)


def _seq_kernel(r_ref, k_ref, v_ref, l_ref, u_ref, sp_ref, spl_ref, past_ref, s0_ref,
                mu_ref, mul_ref, w0_ref, a0_ref, kkw_ref, ka_ref, wd_ref, wa_ref, wg_ref,
                pw_ref, ps_ref, gain_ref, bias_ref, rk_ref, seg_ref, tri_ref,
                ya_ref, yb_ref, g_ref, ns_ref, nsl_ref, np_ref, s_ref,
                carry, carry_l, carry_p, *, prompt_tiles, tiles_p, tiles_s):
    c = CHUNK
    i = pl.program_id(0)
    in_prompt = i < prompt_tiles
    t = jnp.where(in_prompt, i % tiles_p, (i - prompt_tiles) % tiles_s)
    pos0 = jnp.where(in_prompt, 0, PAST_LEN) + t * c

    @pl.when(t == 0)
    def _():
        carry[0:3, :] = sp_ref[0]
        carry_l[0:1, :] = spl_ref[0]
        carry_p[...] = past_ref[0]
        s_ref[...] = s0_ref[...]

    u = u_ref[...]
    ext = jnp.concatenate([carry_p[...], u], axis=0)
    carry_p[...] = ext[c:c + POOL_PAD]
    np_ref[0] = ext[c:c + POOL_PAD]
    pos = pos0 + lax.broadcasted_iota(jnp.int32, (c, 1), 0)
    outs = []
    for gi, win in enumerate(POOL_WINDOWS):
        sl = slice(gi * POOL_GC, (gi + 1) * POOL_GC)
        sw = ext[:, sl]
        sh = 1
        while sh < win:
            sw = sw + pltpu.roll(sw, sh, axis=0)
            sh *= 2
        cnt = jnp.minimum(pos + 1, win).astype(F32)
        dlt = sw[POOL_PAD:] / cnt - u[:, sl]
        outs.append(jnp.dot(dlt.astype(BF16), pw_ref[gi], preferred_element_type=F32))
    ya_ref[...] = (jnp.concatenate(outs, axis=-1) * ps_ref[...]).astype(BF16)

    r_in = r_ref[...]
    k_in = k_ref[...]
    v_in = v_ref[...]
    lo_in = l_ref[...]
    r = _shift_mix(r_in, carry[0:1, :], mu_ref[0:1, :])
    xk = _shift_mix(k_in, carry[1:2, :], mu_ref[1:2, :])
    v = _shift_mix(v_in, carry[2:3, :], mu_ref[2:3, :])
    xl = _shift_mix(lo_in, carry_l[0:1, :], mul_ref[...])
    last = jnp.concatenate([r_in[c - 1:c, :], k_in[c - 1:c, :], v_in[c - 1:c, :]], axis=0)
    carry[0:3, :] = last
    carry_l[0:1, :] = lo_in[c - 1:c, :]
    ns_ref[0] = last
    nsl_ref[0] = lo_in[c - 1:c, :]

    seg = seg_ref[...]
    xda = xl[:, 0:LANES]
    xg = xl[:, LANES:LANES + LORA_GATE_PAD]
    zd = jnp.dot(jnp.tanh(xda).astype(BF16), wd_ref[...], preferred_element_type=F32)
    logw = -jnp.exp(-_softplus(-(w0_ref[...] + zd)) - 0.5)
    za = jnp.dot(xda.astype(BF16), wa_ref[...], preferred_element_type=F32)
    a = jax.nn.sigmoid(a0_ref[...] + za)
    g_ref[...] = jnp.dot(jax.nn.sigmoid(xg).astype(BF16), wg_ref[...], preferred_element_type=F32)
    kkr = xk * kkw_ref[...]
    kk = kkr / jnp.maximum(jnp.sqrt(_segsum(kkr * kkr, seg)), 1e-12)
    k = xk * (1.0 + (a - 1.0) * ka_ref[...])
    bb = kk * a

    hi, lo = _split2(logw)
    tri = tri_ref[...]
    linc = jnp.dot(tri, hi, preferred_element_type=F32) + jnp.dot(tri, lo, preferred_element_type=F32)
    lexc = linc - logw
    ltot = linc[c - 1:c, :]
    e_ninc = jnp.exp(-linc)
    kt = kk * jnp.exp(lexc)
    rt = r * jnp.exp(linc)
    bh = bb * e_ninc
    kh = k * e_ninc
    e_rem = jnp.exp(ltot - linc)
    bbar = bb * e_rem
    kbar = k * e_rem
    gtot = jnp.exp(ltot)

    ri = lax.broadcasted_iota(jnp.int32, (2 * c, 2 * c), 0)
    ci = lax.broadcasted_iota(jnp.int32, (2 * c, 2 * c), 1)
    strict = (ci % c) < (ri % c)
    incl = (ci % c) <= (ri % c)
    blk = (ri // SUB) == (ci // SUB)
    eye = ri == ci
    m0 = lax.broadcasted_iota(jnp.int32, (c, LANES), 1) < RW_HEAD

    pairs = range(PAIRS)
    lanes = [slice(p * LANES, (p + 1) * LANES) for p in pairs]
    each = lambda fn, *cols: [fn(*args) for args in zip(*cols)]
    stacked = lambda x: [_stack2(x[:, sl], m0) for sl in lanes]
    kt2, rt2, bh2, kh2, bbar2, kbar2, v2 = (stacked(x) for x in (kt, rt, bh, kh, bbar, kbar, v))

    aa = each(lambda a_, b_, c_, d_: _mm_nt(jnp.concatenate([a_, b_], axis=0), jnp.concatenate([c_, d_], axis=0)),
              kt2, rt2, bh2, kh2)
    lp = [jnp.where(strict, x[0:2 * c, 0:2 * c], 0.0) for x in aa]
    akp = [jnp.where(strict, x[0:2 * c, 2 * c:4 * c], 0.0) for x in aa]
    arb = [jnp.where(incl, x[2 * c:4 * c, 0:2 * c], 0.0) for x in aa]
    ark = [jnp.where(incl, x[2 * c:4 * c, 2 * c:4 * c], 0.0) for x in aa]

    ld = [jnp.where(blk, x, 0.0) for x in lp]
    lo_ = each(lambda x, y_: x - y_, lp, ld)
    p2 = each(_mm, ld, ld)
    p4 = each(_mm, p2, p2)
    p8 = each(_mm, p4, p4)
    e1 = each(lambda a_, b_, m: a_ - b_ - m, p2, ld, each(_mm, ld, p2))
    e2 = each(lambda a_, b_, m: a_ + b_ + m, e1, p4, each(_mm, e1, p4))
    dm = each(lambda a_, b_, m: a_ + b_ + m, e2, p8, each(_mm, e2, p8))
    n1 = each(lambda a_, m: a_ + m, lo_, each(_mm, dm, lo_))
    n2 = each(_mm, n1, n1)
    f = each(lambda a_, b_, m: a_ - b_ - m, n2, n1, each(_mm, n1, n2))
    tm = each(lambda a_, b_, m: a_ + b_ + m, f, dm, each(_mm, f, dm))

    akv = each(_mm, akp, v2)
    uu = each(lambda a_, b_: jnp.concatenate([a_, b_], axis=1), kt2, akv)
    pq = each(lambda a_, m: a_ + m, uu, each(_mm, tm, uu))
    rbpq = each(_mm, arb, pq)
    ry = each(lambda a_, m: a_ - m[:, 0:LANES], rt2, rbpq)
    y0 = each(lambda m, n_: m - n_[:, LANES:2 * LANES], each(_mm, ark, v2), rbpq)
    btpq = each(_mm_tn, bbar2, pq)
    g2 = [jnp.where(eye, gtot[:, sl], 0.0) - m[:, 0:LANES] for sl, m in zip(lanes, btpq)]
    h2 = each(lambda m, n_: m - n_[:, LANES:2 * LANES], each(_mm_tn, kbar2, v2), btpq)

    st = [s_ref[0, p] for p in pairs]
    y2 = each(lambda a_, b_, o: _mm(a_, b_) + o, ry, st, y0)
    s_new = each(lambda a_, b_, o: _mm_hilo(a_, b_) + o, g2, st, h2)
    for p in pairs:
        s_ref[0, p] = s_new[p]
    y = jnp.concatenate([x[0:c] + x[c:2 * c] for x in y2], axis=-1)

    inv_n = 1.0 / RW_HEAD
    mean = _segsum(y, seg) * inv_n
    yc = y - mean
    var = _segsum(yc * yc, seg) * inv_n
    yn = yc * lax.rsqrt(var + GN_EPS) * gain_ref[...] + bias_ref[...]
    bonus = _segsum(r * k * rk_ref[...], seg)
    yb_ref[...] = yn + bonus * v


def seq_mixers(proj, shift_rkv, shift_lora, pool_past, s0_blk, lw, *, prompt_batch, prompt_len, sample_len):
    c = CHUNK
    n_seq = shift_rkv.shape[0]
    tiles_p = prompt_len // c
    tiles_s = sample_len // c
    prompt_tiles = prompt_batch * tiles_p
    m = proj.shape[0]
    n_tiles = m // c
    seq = lambda i: jnp.where(i < prompt_tiles, i // tiles_p, prompt_batch + (i - prompt_tiles) // tiles_s)
    col = lambda width, cb: pl.BlockSpec((c, width), lambda i: (i, cb))
    per_seq = lambda *shape: pl.BlockSpec((1,) + shape, lambda i: (seq(i),) + tuple(0 for _ in shape))
    full = lambda *shape: pl.BlockSpec(shape, lambda i: tuple(0 for _ in shape))
    tri = (jnp.arange(c)[:, None] >= jnp.arange(c)[None, :]).astype(BF16)
    row = lambda v: v.reshape(1, -1)
    return pl.pallas_call(
        functools.partial(_seq_kernel, prompt_tiles=prompt_tiles, tiles_p=tiles_p, tiles_s=tiles_s),
        out_shape=(jax.ShapeDtypeStruct((m, POOL_WIDTH), BF16),
                   jax.ShapeDtypeStruct((m, RW_WIDTH), F32),
                   jax.ShapeDtypeStruct((m, RW_WIDTH), F32),
                   jax.ShapeDtypeStruct((n_seq, 3, RW_WIDTH), F32),
                   jax.ShapeDtypeStruct((n_seq, 1, LORA_PAD), F32),
                   jax.ShapeDtypeStruct((n_seq, POOL_PAD, POOL_WIDTH), F32),
                   jax.ShapeDtypeStruct((n_seq, PAIRS, LANES, LANES), F32)),
        grid=(n_tiles,),
        in_specs=[
            col(RW_WIDTH, 0), col(RW_WIDTH, 1), col(RW_WIDTH, 2), col(LORA_PAD, COL_LORA // LORA_PAD),
            col(POOL_WIDTH, COL_POOL // POOL_WIDTH),
            per_seq(3, RW_WIDTH), per_seq(1, LORA_PAD), per_seq(POOL_PAD, POOL_WIDTH), per_seq(PAIRS, LANES, LANES),
            full(3, RW_WIDTH), full(1, LORA_PAD),
            full(1, RW_WIDTH), full(1, RW_WIDTH), full(1, RW_WIDTH), full(1, RW_WIDTH),
            full(LANES, RW_WIDTH), full(LANES, RW_WIDTH), full(LORA_GATE_PAD, RW_WIDTH),
            full(POOL_GROUPS, POOL_GC, POOL_GC), full(1, POOL_WIDTH),
            full(1, RW_WIDTH), full(1, RW_WIDTH), full(1, RW_WIDTH),
            full(LANES, LANES), full(c, c),
        ],
        out_specs=(col(POOL_WIDTH, 0), col(RW_WIDTH, 0), col(RW_WIDTH, 0),
                   per_seq(3, RW_WIDTH), per_seq(1, LORA_PAD), per_seq(POOL_PAD, POOL_WIDTH),
                   per_seq(PAIRS, LANES, LANES)),
        scratch_shapes=[pltpu.VMEM((8, RW_WIDTH), F32), pltpu.VMEM((8, LORA_PAD), F32),
                        pltpu.VMEM((POOL_PAD, POOL_WIDTH), F32)],
        compiler_params=_params(("arbitrary",)),
        name="seq_mixers",
    )(proj, proj, proj, proj, proj, shift_rkv, shift_lora, pool_past, s0_blk,
      lw["mu_rkv"], lw["mu_lora"], lw["w0"], lw["a0"], lw["k_k"], lw["k_a"], lw["wd"], lw["wa"], lw["wg"],
      lw["pool_w"], row(lw["pool_scale"]), row(lw["gn_gain"]), row(lw["gn_bias"]), row(lw["r_k"]), lw["seg"], tri)


def _state_to_blocks(s0):
    b = s0.shape[0]
    st = jnp.swapaxes(s0, -1, -2).reshape(b, PAIRS, 2, RW_HEAD, RW_HEAD)
    return jnp.einsum('bphkv,hg->bphkgv', st, jnp.eye(2, dtype=s0.dtype)).reshape(b, PAIRS, LANES, LANES)


def _blocks_to_state(sb):
    b = sb.shape[0]
    x = sb.reshape(b, PAIRS, 2, RW_HEAD, 2, RW_HEAD)
    d = jnp.stack([x[:, :, 0, :, 0, :], x[:, :, 1, :, 1, :]], axis=2)
    return jnp.swapaxes(d.reshape(b, RW_HEADS, RW_HEAD, RW_HEAD), -1, -2)


def _stacked_weights(w_in, pool_w, w_decay_up, w_aaa_up, w_gate_up, proj_pool, proj_rwkv, w_out,
                     w_ffn_gate, w_ffn_up, w_ffn_down, w_ple_gate, w_ple_proj):
    c_rw = POOL_WIDTH
    c_lora = POOL_WIDTH + 3 * RW_WIDTH
    c_gp = POOL_WIDTH + RW_PROJ
    c_gr = c_gp + D_MODEL
    w_in_r = jnp.concatenate([
        w_in[:, :, c_rw:c_lora], w_in[:, :, :POOL_WIDTH],
        w_in[:, :, c_lora:c_gp], jnp.zeros((DEPTH, D_MODEL, LORA_PAD - LORA), F32),
        w_in[:, :, c_gp:c_gr], w_in[:, :, c_gr:]], axis=2).astype(BF16)
    zpad = lambda rows: jnp.zeros((DEPTH, rows, RW_WIDTH), F32)
    return dict(
        w_in=w_in_r, pool_w=pool_w.astype(BF16),
        wd=jnp.concatenate([w_decay_up, zpad(LANES - DECAY_LORA)], axis=1).astype(BF16),
        wa=jnp.concatenate([zpad(DECAY_LORA), w_aaa_up], axis=1).astype(BF16),
        wg=jnp.concatenate([w_gate_up, zpad(LORA_GATE_PAD - GATE_LORA)], axis=1).astype(BF16),
        proj_pool=proj_pool.astype(BF16), proj_rwkv=proj_rwkv.astype(BF16), w_out=w_out.astype(BF16),
        w_ffn_gate=w_ffn_gate.astype(BF16), w_ffn_up=w_ffn_up.astype(BF16), w_ffn_down=w_ffn_down.astype(BF16),
        w_ple_gate=w_ple_gate.astype(BF16), w_ple_proj=w_ple_proj.astype(BF16))


def kernel(x_prompt, x_sample, state_shift, state_pool, state_wkv, p_prompt, p_sample, norm_mix, w_in, mu_shift, pool_w, pool_scale, w0, w_decay_up, a0, w_aaa_up, w_gate_up, k_k, k_a, r_k, gn_gain, gn_bias, proj_pool, proj_rwkv, w_out, norm_ffn, w_ffn_gate, w_ffn_up, w_ffn_down, norm_ple, w_ple_gate, w_ple_proj, norm_final):
    bp, tp, _ = x_prompt.shape
    bs, ts, _ = x_sample.shape
    mp = bp * tp
    ms = bs * ts
    x = jnp.concatenate([x_prompt.reshape(mp, D_MODEL), x_sample.reshape(ms, D_MODEL)], axis=0)
    sw = _stacked_weights(w_in, pool_w, w_decay_up, w_aaa_up, w_gate_up, proj_pool, proj_rwkv, w_out,
                          w_ffn_gate, w_ffn_up, w_ffn_down, w_ple_gate, w_ple_proj)
    lane = jnp.arange(LANES) // RW_HEAD
    seg = (lane[:, None] == lane[None, :]).astype(BF16)
    lead = lambda a: jnp.concatenate([jnp.zeros((DEPTH, bp) + a.shape[2:], a.dtype), a], axis=1)
    shift_all = lead(state_shift)
    shift_rkv = shift_all[:, :, :3 * RW_WIDTH].reshape(DEPTH, bp + bs, 3, RW_WIDTH)
    shift_lora = jnp.pad(shift_all[:, :, 3 * RW_WIDTH:], ((0, 0), (0, 0), (0, LORA_PAD - LORA)))
    shift_lora = shift_lora.reshape(DEPTH, bp + bs, 1, LORA_PAD)
    pool_all = jnp.pad(lead(state_pool), ((0, 0), (0, 0), (POOL_PAD - POOL_PAST, 0), (0, 0)))
    wkv_all = lead(state_wkv)
    row = lambda v: v.reshape(1, -1)
    new_shift, new_pool, new_wkv = [], [], []
    for i in range(DEPTH):
        mu = mu_shift[i]
        lw = dict(
            mu_rkv=mu[:3 * RW_WIDTH].reshape(3, RW_WIDTH),
            mu_lora=jnp.pad(mu[3 * RW_WIDTH:], (0, LORA_PAD - LORA)).reshape(1, LORA_PAD),
            pool_w=sw["pool_w"][i], pool_scale=pool_scale[i],
            w0=row(w0[i]), a0=row(a0[i]), k_k=row(k_k[i]), k_a=row(k_a[i]),
            wd=sw["wd"][i], wa=sw["wa"][i], wg=sw["wg"][i], seg=seg,
            r_k=r_k[i].reshape(RW_WIDTH), gn_gain=gn_gain[i], gn_bias=gn_bias[i])
        proj, gates = in_proj(x, norm_mix[i], sw["w_in"], i)
        ya, yb, g, ns_rkv, ns_lora, pool_rows, s_blk = seq_mixers(
            proj, shift_rkv[i], shift_lora[i], pool_all[i], _state_to_blocks(wkv_all[i]), lw,
            prompt_batch=bp, prompt_len=tp, sample_len=ts)
        new_shift.append(jnp.concatenate([ns_rkv.reshape(bp + bs, 3 * RW_WIDTH), ns_lora[:, 0, :LORA]], axis=1))
        new_pool.append(pool_rows[:, POOL_PAD - POOL_PAST:])
        new_wkv.append(_blocks_to_state(s_blk))
        x = merge_out(ya, yb, g, gates, x, sw["proj_pool"], sw["proj_rwkv"], sw["w_out"], i)
        x = ffn(x, norm_ffn[i], sw["w_ffn_gate"], sw["w_ffn_up"], sw["w_ffn_down"], i)
        p = jnp.concatenate([p_prompt[i].reshape(mp, PLE_DIM), p_sample[i].reshape(ms, PLE_DIM)], axis=0)
        x = ple_update(x, norm_ple[i], sw["w_ple_gate"], p, sw["w_ple_proj"], i)
    y_prompt = final_norm(x, norm_final, row0=0, rows=mp).reshape(bp, tp, D_MODEL)
    y_sample = final_norm(x, norm_final, row0=mp, rows=ms).reshape(bs, ts, D_MODEL)
    new_shift, new_pool, new_wkv = jnp.stack(new_shift), jnp.stack(new_pool), jnp.stack(new_wkv)
    return (y_prompt, y_sample, new_shift[:, :bp], new_pool[:, :bp], new_wkv[:, :bp],
            new_shift[:, bp:], new_pool[:, bp:], new_wkv[:, bp:])
```

```python
import functools

import jax
import jax.numpy as jnp
from jax import lax
from jax.experimental import pallas as pl
from jax.experimental.pallas import tpu as pltpu

F32 = jnp.float32
BF16 = jnp.bfloat16

D_MODEL = 2048
DEPTH = 4
PAST_LEN = 1024
PLE_DIM = 256
POOL_WINDOWS = (2, 4, 8, 16)
POOL_GROUPS = 4
POOL_WIDTH = D_MODEL // 2
POOL_GC = POOL_WIDTH // POOL_GROUPS
POOL_PAST = max(POOL_WINDOWS) - 1
RW_WIDTH = D_MODEL // 2
RW_HEAD = 64
RW_HEADS = RW_WIDTH // RW_HEAD
DECAY_LORA = 64
AAA_LORA = 64
GATE_LORA = 160
LORA = DECAY_LORA + AAA_LORA + GATE_LORA
RW_PROJ = 3 * RW_WIDTH + LORA
D_FF = 5632
NORM_EPS = 1e-6
GN_EPS = 64e-5

LANES = 128
POOL_PAD = 16
LORA_PAD = 512
LORA_GATE_PAD = 256
PAIRS = RW_WIDTH // LANES
CHUNK = 64
SUB = 16
COL_RKV = 0
COL_POOL = 3 * RW_WIDTH
COL_LORA = COL_POOL + POOL_WIDTH
MAIN_COLS = COL_LORA + LORA_PAD
GATE_COLS = 2 * D_MODEL

TN = 512
TM_WIDE = 1536
TM_ROW = 768
TM_MERGE = 384
VMEM_LIMIT = 56 * 1024 * 1024


def _params(sem):
    return pltpu.CompilerParams(dimension_semantics=sem, vmem_limit_bytes=VMEM_LIMIT)


def _rms(x, g):
    ms = jnp.mean(x * x, axis=-1, keepdims=True)
    return x * lax.rsqrt(ms + NORM_EPS) * g


def _wspec(k, n, layer, col_of):
    return pl.BlockSpec((None, k, n), lambda *idx: (layer, 0, col_of(*idx)))


def _in_proj_kernel(x_ref, g_ref, w_ref, main_ref, gates_ref, h_scr):
    @pl.when(pl.program_id(1) == 0)
    def _():
        h_scr[...] = _rms(x_ref[...], g_ref[...]).astype(BF16)

    acc = jnp.dot(h_scr[...], w_ref[...], preferred_element_type=F32)
    main_ref[...] = acc
    gates_ref[...] = acc.astype(BF16)


def in_proj(x, g, w, layer):
    m, k = x.shape
    tm = TM_WIDE
    main_tiles = MAIN_COLS // TN
    gate_tiles = GATE_COLS // TN
    return pl.pallas_call(
        _in_proj_kernel,
        out_shape=(jax.ShapeDtypeStruct((m, MAIN_COLS + TN), F32), jax.ShapeDtypeStruct((m, GATE_COLS + TN), BF16)),
        grid=(m // tm, main_tiles + gate_tiles),
        in_specs=[
            pl.BlockSpec((tm, k), lambda i, j: (i, 0)),
            pl.BlockSpec((1, k), lambda i, j: (0, 0)),
            _wspec(k, TN, layer, lambda i, j: j),
        ],
        out_specs=(pl.BlockSpec((tm, TN), lambda i, j: (i, jnp.minimum(j, main_tiles))),
                   pl.BlockSpec((tm, TN), lambda i, j: (i, jnp.where(j < main_tiles, gate_tiles, j - main_tiles)))),
        scratch_shapes=[pltpu.VMEM((tm, k), BF16)],
        compiler_params=_params(("parallel", "arbitrary")),
        name="in_proj",
    )(x, g.reshape(1, k), w)


def _merge_out_kernel(ya_ref, yb_ref, g_ref, gates_ref, x_ref, wp_ref, wr_ref, wo_ref, o_ref, ybg_scr, m_scr):
    n = o_ref.shape[1]
    ybg_scr[...] = (yb_ref[...] * g_ref[...]).astype(BF16)
    for c in range(n // TN):
        sl = slice(c * TN, (c + 1) * TN)
        sr = slice(n + c * TN, n + (c + 1) * TN)
        a = jnp.dot(ya_ref[...], wp_ref[:, sl], preferred_element_type=F32)
        b = jnp.dot(ybg_scr[...], wr_ref[:, sl], preferred_element_type=F32)
        gate_a = jax.nn.sigmoid(gates_ref[:, sl].astype(F32))
        gate_b = jax.nn.sigmoid(gates_ref[:, sr].astype(F32))
        m_scr[:, sl] = (gate_a * a + gate_b * b).astype(BF16)
    for c in range(n // TN):
        sl = slice(c * TN, (c + 1) * TN)
        o_ref[:, sl] = x_ref[:, sl] + jnp.dot(m_scr[...], wo_ref[:, sl], preferred_element_type=F32)


def merge_out(ya, yb, g, gates, x, wp, wr, wo, layer):
    m, n = x.shape
    tm = TM_MERGE
    resident = lambda k_: pl.BlockSpec((None, k_, n), lambda i: (layer, 0, 0), pipeline_mode=pl.Buffered(1))
    return pl.pallas_call(
        _merge_out_kernel,
        out_shape=jax.ShapeDtypeStruct((m, n), F32),
        grid=(m // tm,),
        in_specs=[
            pl.BlockSpec((tm, POOL_WIDTH), lambda i: (i, 0)),
            pl.BlockSpec((tm, RW_WIDTH), lambda i: (i, 0)),
            pl.BlockSpec((tm, RW_WIDTH), lambda i: (i, 0)),
            pl.BlockSpec((tm, GATE_COLS), lambda i: (i, 0)),
            pl.BlockSpec((tm, n), lambda i: (i, 0)),
            resident(POOL_WIDTH), resident(RW_WIDTH), resident(n),
        ],
        out_specs=pl.BlockSpec((tm, n), lambda i: (i, 0)),
        scratch_shapes=[pltpu.VMEM((tm, RW_WIDTH), BF16), pltpu.VMEM((tm, n), BF16)],
        compiler_params=_params(("parallel",)),
        name="merge_out",
    )(ya, yb, g, gates, x, wp, wr, wo)


def _ffn_kernel(x_ref, g_ref, wg_ref, wu_ref, wd_ref, o_ref, h_scr):
    @pl.when(pl.program_id(1) == 0)
    def _():
        x = x_ref[...]
        h_scr[...] = _rms(x, g_ref[...]).astype(BF16)
        o_ref[...] = x

    h = h_scr[...]
    gate = jnp.dot(h, wg_ref[...], preferred_element_type=F32)
    up = jnp.dot(h, wu_ref[...], preferred_element_type=F32)
    act = (gate * jax.nn.sigmoid(gate) * up).astype(BF16)
    o_ref[...] += jnp.dot(act, wd_ref[...], preferred_element_type=F32)


def ffn(x, g, wg, wu, wd, layer):
    m, k = x.shape
    f = wg.shape[2]
    tm = TM_ROW
    return pl.pallas_call(
        _ffn_kernel,
        out_shape=jax.ShapeDtypeStruct((m, k), F32),
        grid=(m // tm, f // TN),
        in_specs=[
            pl.BlockSpec((tm, k), lambda i, j: (i, 0)),
            pl.BlockSpec((1, k), lambda i, j: (0, 0)),
            _wspec(k, TN, layer, lambda i, j: j),
            _wspec(k, TN, layer, lambda i, j: j),
            pl.BlockSpec((None, TN, k), lambda i, j: (layer, j, 0)),
        ],
        out_specs=pl.BlockSpec((tm, k), lambda i, j: (i, 0)),
        scratch_shapes=[pltpu.VMEM((tm, k), BF16)],
        compiler_params=_params(("parallel", "arbitrary")),
        name="ffn",
    )(x, g.reshape(1, k), wg, wu, wd)


def _ple_kernel(x_ref, g_ref, wg_ref, p_ref, wp_ref, o_ref, h_scr):
    n = o_ref.shape[1]
    h_scr[...] = _rms(x_ref[...], g_ref[...]).astype(BF16)
    pb = p_ref[...].astype(BF16)
    for c in range(n // TN):
        sl = slice(c * TN, (c + 1) * TN)
        gate = jnp.dot(h_scr[...], wg_ref[:, sl], preferred_element_type=F32)
        emb = jnp.dot(pb, wp_ref[:, sl], preferred_element_type=F32)
        o_ref[:, sl] = x_ref[:, sl] + jax.nn.sigmoid(gate) * emb


def ple_update(x, g, wg, p, wp, layer):
    m, k = x.shape
    n = wg.shape[2]
    kp = p.shape[1]
    tm = TM_ROW
    return pl.pallas_call(
        _ple_kernel,
        out_shape=jax.ShapeDtypeStruct((m, n), F32),
        grid=(m // tm,),
        in_specs=[
            pl.BlockSpec((tm, k), lambda i: (i, 0)),
            pl.BlockSpec((1, k), lambda i: (0, 0)),
            _wspec(k, n, layer, lambda i: 0),
            pl.BlockSpec((tm, kp), lambda i: (i, 0)),
            _wspec(kp, n, layer, lambda i: 0),
        ],
        out_specs=pl.BlockSpec((tm, n), lambda i: (i, 0)),
        scratch_shapes=[pltpu.VMEM((tm, k), BF16)],
        compiler_params=_params(("parallel",)),
        name="ple_update",
    )(x, g.reshape(1, k), wg, p, wp)


def _final_norm_kernel(x_ref, g_ref, o_ref):
    o_ref[...] = _rms(x_ref[...], g_ref[...])


def final_norm(x, g, *, row0, rows):
    k = x.shape[1]
    tm = 512
    rb0 = row0 // tm
    return pl.pallas_call(
        _final_norm_kernel,
        out_shape=jax.ShapeDtypeStruct((rows, k), F32),
        grid=(rows // tm,),
        in_specs=[pl.BlockSpec((tm, k), lambda i: (rb0 + i, 0)), pl.BlockSpec((1, k), lambda i: (0, 0))],
        out_specs=pl.BlockSpec((tm, k), lambda i: (i, 0)),
        compiler_params=_params(("parallel",)),
        name="final_norm",
    )(x, g.reshape(1, k))


def _shift_mix(cur, carry_row, mu):
    rolled = pltpu.roll(cur, 1, axis=0)
    row = lax.broadcasted_iota(jnp.int32, cur.shape, 0)
    prev = jnp.where(row == 0, carry_row, rolled)
    return cur + (prev - cur) * mu


def _softplus(x):
    return jnp.maximum(x, 0.0) + jnp.log1p(jnp.exp(-jnp.abs(x)))


def _mm(a, b):
    return jnp.dot(a.astype(BF16), b.astype(BF16), preferred_element_type=F32)


def _mm_nt(a, b):
    return lax.dot_general(a.astype(BF16), b.astype(BF16), (((1,), (1,)), ((), ())),
                           preferred_element_type=F32)


def _mm_tn(a, b):
    return lax.dot_general(a.astype(BF16), b.astype(BF16), (((0,), (0,)), ((), ())),
                           preferred_element_type=F32)


def _split2(x):
    hi = x.astype(BF16)
    lo = (x - hi.astype(F32)).astype(BF16)
    return hi, lo


def _mm_hilo(a, b):
    ah, al = _split2(a)
    bh, bl = _split2(b)
    d = lambda x, y: jnp.dot(x, y, preferred_element_type=F32)
    return d(ah, bh) + d(ah, bl) + d(al, bh)


def _segsum(x, seg):
    xb = x.astype(BF16)
    parts = [jnp.dot(xb[:, c * LANES:(c + 1) * LANES], seg, preferred_element_type=F32) for c in range(PAIRS)]
    return jnp.concatenate(parts, axis=-1)


def _stack2(x, m0):
    return jnp.concatenate([jnp.where(m0, x, 0.0), jnp.where(m0, 0.0, x)], axis=0)


def _seq_kernel(r_ref, k_ref, v_ref, l_ref, u_ref, sp_ref, spl_ref, past_ref, s0_ref,
                mu_ref, mul_ref, w0_ref, a0_ref, kkw_ref, ka_ref, wd_ref, wa_ref, wg_ref,
                pw_ref, ps_ref, gain_ref, bias_ref, rk_ref, seg_ref, tri_ref,
                ya_ref, yb_ref, g_ref, ns_ref, nsl_ref, np_ref, s_ref,
                carry, carry_l, carry_p, *, prompt_tiles, tiles_p, tiles_s):
    c = CHUNK
    i = pl.program_id(0)
    in_prompt = i < prompt_tiles
    t = jnp.where(in_prompt, i % tiles_p, (i - prompt_tiles) % tiles_s)
    pos0 = jnp.where(in_prompt, 0, PAST_LEN) + t * c

    @pl.when(t == 0)
    def _():
        carry[0:3, :] = sp_ref[0]
        carry_l[0:1, :] = spl_ref[0]
        carry_p[...] = past_ref[0]
        s_ref[...] = s0_ref[...]

    u = u_ref[...]
    ext = jnp.concatenate([carry_p[...], u], axis=0)
    carry_p[...] = ext[c:c + POOL_PAD]
    np_ref[0] = ext[c:c + POOL_PAD]
    pos = pos0 + lax.broadcasted_iota(jnp.int32, (c, 1), 0)
    outs = []
    for gi, win in enumerate(POOL_WINDOWS):
        sl = slice(gi * POOL_GC, (gi + 1) * POOL_GC)
        sw = ext[:, sl]
        sh = 1
        while sh < win:
            sw = sw + pltpu.roll(sw, sh, axis=0)
            sh *= 2
        cnt = jnp.minimum(pos + 1, win).astype(F32)
        dlt = sw[POOL_PAD:] / cnt - u[:, sl]
        outs.append(jnp.dot(dlt.astype(BF16), pw_ref[gi], preferred_element_type=F32))
    ya_ref[...] = (jnp.concatenate(outs, axis=-1) * ps_ref[...]).astype(BF16)

    r_in = r_ref[...]
    k_in = k_ref[...]
    v_in = v_ref[...]
    lo_in = l_ref[...]
    r = _shift_mix(r_in, carry[0:1, :], mu_ref[0:1, :])
    xk = _shift_mix(k_in, carry[1:2, :], mu_ref[1:2, :])
    v = _shift_mix(v_in, carry[2:3, :], mu_ref[2:3, :])
    xl = _shift_mix(lo_in, carry_l[0:1, :], mul_ref[...])
    last = jnp.concatenate([r_in[c - 1:c, :], k_in[c - 1:c, :], v_in[c - 1:c, :]], axis=0)
    carry[0:3, :] = last
    carry_l[0:1, :] = lo_in[c - 1:c, :]
    ns_ref[0] = last
    nsl_ref[0] = lo_in[c - 1:c, :]

    seg = seg_ref[...]
    xda = xl[:, 0:LANES]
    xg = xl[:, LANES:LANES + LORA_GATE_PAD]
    zd = jnp.dot(jnp.tanh(xda).astype(BF16), wd_ref[...], preferred_element_type=F32)
    logw = -jnp.exp(-_softplus(-(w0_ref[...] + zd)) - 0.5)
    za = jnp.dot(xda.astype(BF16), wa_ref[...], preferred_element_type=F32)
    a = jax.nn.sigmoid(a0_ref[...] + za)
    g_ref[...] = jnp.dot(jax.nn.sigmoid(xg).astype(BF16), wg_ref[...], preferred_element_type=F32)
    kkr = xk * kkw_ref[...]
    kk = kkr / jnp.maximum(jnp.sqrt(_segsum(kkr * kkr, seg)), 1e-12)
    k = xk * (1.0 + (a - 1.0) * ka_ref[...])
    bb = kk * a

    hi, lo = _split2(logw)
    tri = tri_ref[...]
    linc = jnp.dot(tri, hi, preferred_element_type=F32) + jnp.dot(tri, lo, preferred_element_type=F32)
    lexc = linc - logw
    ltot = linc[c - 1:c, :]
    e_ninc = jnp.exp(-linc)
    kt = kk * jnp.exp(lexc)
    rt = r * jnp.exp(linc)
    bh = bb * e_ninc
    kh = k * e_ninc
    e_rem = jnp.exp(ltot - linc)
    bbar = bb * e_rem
    kbar = k * e_rem
    gtot = jnp.exp(ltot)

    ri = lax.broadcasted_iota(jnp.int32, (2 * c, 2 * c), 0)
    ci = lax.broadcasted_iota(jnp.int32, (2 * c, 2 * c), 1)
    strict = (ci % c) < (ri % c)
    incl = (ci % c) <= (ri % c)
    blk = (ri // SUB) == (ci // SUB)
    eye = ri == ci
    m0 = lax.broadcasted_iota(jnp.int32, (c, LANES), 1) < RW_HEAD

    pairs = range(PAIRS)
    lanes = [slice(p * LANES, (p + 1) * LANES) for p in pairs]
    each = lambda fn, *cols: [fn(*args) for args in zip(*cols)]
    stacked = lambda x: [_stack2(x[:, sl], m0) for sl in lanes]
    kt2, rt2, bh2, kh2, bbar2, kbar2, v2 = (stacked(x) for x in (kt, rt, bh, kh, bbar, kbar, v))

    aa = each(lambda a_, b_, c_, d_: _mm_nt(jnp.concatenate([a_, b_], axis=0), jnp.concatenate([c_, d_], axis=0)),
              kt2, rt2, bh2, kh2)
    lp = [jnp.where(strict, x[0:2 * c, 0:2 * c], 0.0) for x in aa]
    akp = [jnp.where(strict, x[0:2 * c, 2 * c:4 * c], 0.0) for x in aa]
    arb = [jnp.where(incl, x[2 * c:4 * c, 0:2 * c], 0.0) for x in aa]
    ark = [jnp.where(incl, x[2 * c:4 * c, 2 * c:4 * c], 0.0) for x in aa]

    ld = [jnp.where(blk, x, 0.0) for x in lp]
    lo_ = each(lambda x, y_: x - y_, lp, ld)
    p2 = each(_mm, ld, ld)
    p4 = each(_mm, p2, p2)
    p8 = each(_mm, p4, p4)
    e1 = each(lambda a_, b_, m: a_ - b_ - m, p2, ld, each(_mm, ld, p2))
    e2 = each(lambda a_, b_, m: a_ + b_ + m, e1, p4, each(_mm, e1, p4))
    dm = each(lambda a_, b_, m: a_ + b_ + m, e2, p8, each(_mm, e2, p8))
    n1 = each(lambda a_, m: a_ + m, lo_, each(_mm, dm, lo_))
    n2 = each(_mm, n1, n1)
    f = each(lambda a_, b_, m: a_ - b_ - m, n2, n1, each(_mm, n1, n2))
    tm = each(lambda a_, b_, m: a_ + b_ + m, f, dm, each(_mm, f, dm))

    akv = each(_mm, akp, v2)
    uu = each(lambda a_, b_: jnp.concatenate([a_, b_], axis=1), kt2, akv)
    pq = each(lambda a_, m: a_ + m, uu, each(_mm, tm, uu))
    rbpq = each(_mm, arb, pq)
    ry = each(lambda a_, m: a_ - m[:, 0:LANES], rt2, rbpq)
    y0 = each(lambda m, n_: m - n_[:, LANES:2 * LANES], each(_mm, ark, v2), rbpq)
    btpq = each(_mm_tn, bbar2, pq)
    g2 = [jnp.where(eye, gtot[:, sl], 0.0) - m[:, 0:LANES] for sl, m in zip(lanes, btpq)]
    h2 = each(lambda m, n_: m - n_[:, LANES:2 * LANES], each(_mm_tn, kbar2, v2), btpq)

    st = [s_ref[0, p] for p in pairs]
    y2 = each(lambda a_, b_, o: _mm(a_, b_) + o, ry, st, y0)
    s_new = each(lambda a_, b_, o: _mm_hilo(a_, b_) + o, g2, st, h2)
    for p in pairs:
        s_ref[0, p] = s_new[p]
    y = jnp.concatenate([x[0:c] + x[c:2 * c] for x in y2], axis=-1)

    inv_n = 1.0 / RW_HEAD
    mean = _segsum(y, seg) * inv_n
    yc = y - mean
    var = _segsum(yc * yc, seg) * inv_n
    yn = yc * lax.rsqrt(var + GN_EPS) * gain_ref[...] + bias_ref[...]
    bonus = _segsum(r * k * rk_ref[...], seg)
    yb_ref[...] = yn + bonus * v


def seq_mixers(proj, shift_rkv, shift_lora, pool_past, s0_blk, lw, *, prompt_batch, prompt_len, sample_len):
    c = CHUNK
    n_seq = shift_rkv.shape[0]
    tiles_p = prompt_len // c
    tiles_s = sample_len // c
    prompt_tiles = prompt_batch * tiles_p
    m = proj.shape[0]
    n_tiles = m // c
    seq = lambda i: jnp.where(i < prompt_tiles, i // tiles_p, prompt_batch + (i - prompt_tiles) // tiles_s)
    col = lambda width, cb: pl.BlockSpec((c, width), lambda i: (i, cb))
    per_seq = lambda *shape: pl.BlockSpec((1,) + shape, lambda i: (seq(i),) + tuple(0 for _ in shape))
    full = lambda *shape: pl.BlockSpec(shape, lambda i: tuple(0 for _ in shape))
    tri = (jnp.arange(c)[:, None] >= jnp.arange(c)[None, :]).astype(BF16)
    row = lambda v: v.reshape(1, -1)
    return pl.pallas_call(
        functools.partial(_seq_kernel, prompt_tiles=prompt_tiles, tiles_p=tiles_p, tiles_s=tiles_s),
        out_shape=(jax.ShapeDtypeStruct((m, POOL_WIDTH), BF16),
                   jax.ShapeDtypeStruct((m, RW_WIDTH), F32),
                   jax.ShapeDtypeStruct((m, RW_WIDTH), F32),
                   jax.ShapeDtypeStruct((n_seq, 3, RW_WIDTH), F32),
                   jax.ShapeDtypeStruct((n_seq, 1, LORA_PAD), F32),
                   jax.ShapeDtypeStruct((n_seq, POOL_PAD, POOL_WIDTH), F32),
                   jax.ShapeDtypeStruct((n_seq, PAIRS, LANES, LANES), F32)),
        grid=(n_tiles,),
        in_specs=[
            col(RW_WIDTH, 0), col(RW_WIDTH, 1), col(RW_WIDTH, 2), col(LORA_PAD, COL_LORA // LORA_PAD),
            col(POOL_WIDTH, COL_POOL // POOL_WIDTH),
            per_seq(3, RW_WIDTH), per_seq(1, LORA_PAD), per_seq(POOL_PAD, POOL_WIDTH), per_seq(PAIRS, LANES, LANES),
            full(3, RW_WIDTH), full(1, LORA_PAD),
            full(1, RW_WIDTH), full(1, RW_WIDTH), full(1, RW_WIDTH), full(1, RW_WIDTH),
            full(LANES, RW_WIDTH), full(LANES, RW_WIDTH), full(LORA_GATE_PAD, RW_WIDTH),
            full(POOL_GROUPS, POOL_GC, POOL_GC), full(1, POOL_WIDTH),
            full(1, RW_WIDTH), full(1, RW_WIDTH), full(1, RW_WIDTH),
            full(LANES, LANES), full(c, c),
        ],
        out_specs=(col(POOL_WIDTH, 0), col(RW_WIDTH, 0), col(RW_WIDTH, 0),
                   per_seq(3, RW_WIDTH), per_seq(1, LORA_PAD), per_seq(POOL_PAD, POOL_WIDTH),
                   per_seq(PAIRS, LANES, LANES)),
        scratch_shapes=[pltpu.VMEM((8, RW_WIDTH), F32), pltpu.VMEM((8, LORA_PAD), F32),
                        pltpu.VMEM((POOL_PAD, POOL_WIDTH), F32)],
        compiler_params=_params(("arbitrary",)),
        name="seq_mixers",
    )(proj, proj, proj, proj, proj, shift_rkv, shift_lora, pool_past, s0_blk,
      lw["mu_rkv"], lw["mu_lora"], lw["w0"], lw["a0"], lw["k_k"], lw["k_a"], lw["wd"], lw["wa"], lw["wg"],
      lw["pool_w"], row(lw["pool_scale"]), row(lw["gn_gain"]), row(lw["gn_bias"]), row(lw["r_k"]), lw["seg"], tri)


def _state_to_blocks(s0):
    b = s0.shape[0]
    st = jnp.swapaxes(s0, -1, -2).reshape(b, PAIRS, 2, RW_HEAD, RW_HEAD)
    z = jnp.zeros_like(st[:, :, 0])
    top = jnp.concatenate([st[:, :, 0], z], axis=-1)
    bottom = jnp.concatenate([z, st[:, :, 1]], axis=-1)
    return jnp.concatenate([top, bottom], axis=-2)


def _blocks_to_state(sb):
    b = sb.shape[0]
    d = jnp.stack([sb[:, :, :RW_HEAD, :RW_HEAD], sb[:, :, RW_HEAD:, RW_HEAD:]], axis=2)
    return jnp.swapaxes(d.reshape(b, RW_HEADS, RW_HEAD, RW_HEAD), -1, -2)


def _stacked_weights(w_in, pool_w, w_decay_up, w_aaa_up, w_gate_up, proj_pool, proj_rwkv, w_out,
                     w_ffn_gate, w_ffn_up, w_ffn_down, w_ple_gate, w_ple_proj):
    c_rw = POOL_WIDTH
    c_lora = POOL_WIDTH + 3 * RW_WIDTH
    c_gp = POOL_WIDTH + RW_PROJ
    c_gr = c_gp + D_MODEL
    lora_cols = jnp.pad(w_in[:, :, c_lora:c_gp], ((0, 0), (0, 0), (0, LORA_PAD - LORA)))
    w_in_r = jnp.concatenate([
        w_in[:, :, c_rw:c_lora], w_in[:, :, :POOL_WIDTH], lora_cols, w_in[:, :, c_gp:]], axis=2).astype(BF16)
    zpad = lambda rows: jnp.zeros((DEPTH, rows, RW_WIDTH), F32)
    return dict(
        w_in=w_in_r, pool_w=pool_w.astype(BF16),
        wd=jnp.concatenate([w_decay_up, zpad(LANES - DECAY_LORA)], axis=1).astype(BF16),
        wa=jnp.concatenate([zpad(DECAY_LORA), w_aaa_up], axis=1).astype(BF16),
        wg=jnp.concatenate([w_gate_up, zpad(LORA_GATE_PAD - GATE_LORA)], axis=1).astype(BF16),
        proj_pool=proj_pool.astype(BF16), proj_rwkv=proj_rwkv.astype(BF16), w_out=w_out.astype(BF16),
        w_ffn_gate=w_ffn_gate.astype(BF16), w_ffn_up=w_ffn_up.astype(BF16), w_ffn_down=w_ffn_down.astype(BF16),
        w_ple_gate=w_ple_gate.astype(BF16), w_ple_proj=w_ple_proj.astype(BF16))


def kernel(x_prompt, x_sample, state_shift, state_pool, state_wkv, p_prompt, p_sample, norm_mix, w_in, mu_shift, pool_w, pool_scale, w0, w_decay_up, a0, w_aaa_up, w_gate_up, k_k, k_a, r_k, gn_gain, gn_bias, proj_pool, proj_rwkv, w_out, norm_ffn, w_ffn_gate, w_ffn_up, w_ffn_down, norm_ple, w_ple_gate, w_ple_proj, norm_final):
    bp, tp, _ = x_prompt.shape
    bs, ts, _ = x_sample.shape
    mp = bp * tp
    ms = bs * ts
    x = jnp.concatenate([x_prompt.reshape(mp, D_MODEL), x_sample.reshape(ms, D_MODEL)], axis=0)
    sw = _stacked_weights(w_in, pool_w, w_decay_up, w_aaa_up, w_gate_up, proj_pool, proj_rwkv, w_out,
                          w_ffn_gate, w_ffn_up, w_ffn_down, w_ple_gate, w_ple_proj)
    lane = jnp.arange(LANES) // RW_HEAD
    seg = (lane[:, None] == lane[None, :]).astype(BF16)
    lead = lambda a: jnp.concatenate([jnp.zeros((DEPTH, bp) + a.shape[2:], a.dtype), a], axis=1)
    shift_all = lead(state_shift)
    shift_rkv = shift_all[:, :, :3 * RW_WIDTH].reshape(DEPTH, bp + bs, 3, RW_WIDTH)
    shift_lora = jnp.pad(shift_all[:, :, 3 * RW_WIDTH:], ((0, 0), (0, 0), (0, LORA_PAD - LORA)))
    shift_lora = shift_lora.reshape(DEPTH, bp + bs, 1, LORA_PAD)
    pool_all = jnp.pad(lead(state_pool), ((0, 0), (0, 0), (POOL_PAD - POOL_PAST, 0), (0, 0)))
    wkv_all = lead(state_wkv)
    row = lambda v: v.reshape(1, -1)
    new_shift, new_pool, new_wkv = [], [], []
    for i in range(DEPTH):
        mu = mu_shift[i]
        lw = dict(
            mu_rkv=mu[:3 * RW_WIDTH].reshape(3, RW_WIDTH),
            mu_lora=jnp.pad(mu[3 * RW_WIDTH:], (0, LORA_PAD - LORA)).reshape(1, LORA_PAD),
            pool_w=sw["pool_w"][i], pool_scale=pool_scale[i],
            w0=row(w0[i]), a0=row(a0[i]), k_k=row(k_k[i]), k_a=row(k_a[i]),
            wd=sw["wd"][i], wa=sw["wa"][i], wg=sw["wg"][i], seg=seg,
            r_k=r_k[i].reshape(RW_WIDTH), gn_gain=gn_gain[i], gn_bias=gn_bias[i])
        proj, gates = in_proj(x, norm_mix[i], sw["w_in"], i)
        ya, yb, g, ns_rkv, ns_lora, pool_rows, s_blk = seq_mixers(
            proj, shift_rkv[i], shift_lora[i], pool_all[i], _state_to_blocks(wkv_all[i]), lw,
            prompt_batch=bp, prompt_len=tp, sample_len=ts)
        new_shift.append(jnp.concatenate([ns_rkv.reshape(bp + bs, 3 * RW_WIDTH), ns_lora[:, 0, :LORA]], axis=1))
        new_pool.append(pool_rows[:, POOL_PAD - POOL_PAST:])
        new_wkv.append(_blocks_to_state(s_blk))
        x = merge_out(ya, yb, g, gates, x, sw["proj_pool"], sw["proj_rwkv"], sw["w_out"], i)
        x = ffn(x, norm_ffn[i], sw["w_ffn_gate"], sw["w_ffn_up"], sw["w_ffn_down"], i)
        p = jnp.concatenate([p_prompt[i].reshape(mp, PLE_DIM), p_sample[i].reshape(ms, PLE_DIM)], axis=0)
        x = ple_update(x, norm_ple[i], sw["w_ple_gate"], p, sw["w_ple_proj"], i)
    y_prompt = final_norm(x, norm_final, row0=0, rows=mp).reshape(bp, tp, D_MODEL)
    y_sample = final_norm(x, norm_final, row0=mp, rows=ms).reshape(bs, ts, D_MODEL)
    new_shift, new_pool, new_wkv = jnp.stack(new_shift), jnp.stack(new_pool), jnp.stack(new_wkv)
    return (y_prompt, y_sample, new_shift[:, :bp], new_pool[:, :bp], new_wkv[:, :bp],
            new_shift[:, bp:], new_pool[:, bp:], new_wkv[:, bp:])
```

```python
import functools

import jax
import jax.numpy as jnp
from jax import lax
from jax.experimental import pallas as pl
from jax.experimental.pallas import tpu as pltpu

F32 = jnp.float32
BF16 = jnp.bfloat16

D_MODEL = 2048
DEPTH = 4
PAST_LEN = 1024
PLE_DIM = 256
POOL_WINDOWS = (2, 4, 8, 16)
POOL_GROUPS = 4
POOL_WIDTH = D_MODEL // 2
POOL_GC = POOL_WIDTH // POOL_GROUPS
POOL_PAST = max(POOL_WINDOWS) - 1
RW_WIDTH = D_MODEL // 2
RW_HEAD = 64
RW_HEADS = RW_WIDTH // RW_HEAD
DECAY_LORA = 64
AAA_LORA = 64
GATE_LORA = 160
LORA = DECAY_LORA + AAA_LORA + GATE_LORA
RW_PROJ = 3 * RW_WIDTH + LORA
IN_COLS = POOL_WIDTH + RW_PROJ + 2 * D_MODEL
D_FF = 5632
NORM_EPS = 1e-6
GN_EPS = 64e-5

LANES = 128
POOL_PAD = 16
LORA_PAD = 512
LORA_GATE_PAD = 256
PAIRS = RW_WIDTH // LANES
CHUNK = 64
SUB = 16
COL_RKV = 0
COL_POOL = 3 * RW_WIDTH
COL_LORA = COL_POOL + POOL_WIDTH
MAIN_COLS = COL_LORA + LORA_PAD
GATE_COLS = 2 * D_MODEL

TN = 512
TM_WIDE = 1536
TM_ROW = 768
TM_MERGE = 384
TM_PLE = 512
TR_WPREP = 256
VMEM_LIMIT = 56 * 1024 * 1024


def _params(sem):
    return pltpu.CompilerParams(dimension_semantics=sem, vmem_limit_bytes=VMEM_LIMIT)


def _rms(x, g):
    ms = jnp.mean(x * x, axis=-1, keepdims=True)
    return x * lax.rsqrt(ms + NORM_EPS) * g


def _wspec(k, n, layer, col_of):
    return pl.BlockSpec((None, k, n), lambda *idx: (layer, 0, col_of(*idx)))


def _w_in_prep_kernel(w_ref, o_ref):
    c_rw = POOL_WIDTH
    c_lora = POOL_WIDTH + 3 * RW_WIDTH
    c_gp = POOL_WIDTH + RW_PROJ
    rows = w_ref.shape[1]
    o_ref[0, :, COL_RKV:COL_POOL] = w_ref[0, :, c_rw:c_lora].astype(BF16)
    o_ref[0, :, COL_POOL:COL_LORA] = w_ref[0, :, 0:c_rw].astype(BF16)
    lora = jnp.concatenate([w_ref[0, :, c_lora:c_gp], jnp.zeros((rows, LORA_PAD - LORA), F32)], axis=1)
    o_ref[0, :, COL_LORA:MAIN_COLS] = lora.astype(BF16)
    o_ref[0, :, MAIN_COLS:MAIN_COLS + GATE_COLS] = w_ref[0, :, c_gp:IN_COLS].astype(BF16)


def w_in_prep(w_in):
    depth, k, n = w_in.shape
    return pl.pallas_call(
        _w_in_prep_kernel,
        out_shape=jax.ShapeDtypeStruct((depth, k, MAIN_COLS + GATE_COLS), BF16),
        grid=(depth, k // TR_WPREP),
        in_specs=[pl.BlockSpec((1, TR_WPREP, n), lambda l, i: (l, i, 0))],
        out_specs=pl.BlockSpec((1, TR_WPREP, MAIN_COLS + GATE_COLS), lambda l, i: (l, i, 0)),
        compiler_params=_params(("parallel", "parallel")),
        name="w_in_prep",
    )(w_in)


def _in_proj_kernel(x_ref, g_ref, w_ref, main_ref, gates_ref, h_scr):
    @pl.when(pl.program_id(1) == 0)
    def _():
        h_scr[...] = _rms(x_ref[...], g_ref[...]).astype(BF16)

    acc = jnp.dot(h_scr[...], w_ref[...], preferred_element_type=F32)
    main_ref[...] = acc
    gates_ref[...] = acc.astype(BF16)


def in_proj(x, g, w, layer):
    m, k = x.shape
    tm = TM_WIDE
    main_tiles = MAIN_COLS // TN
    gate_tiles = GATE_COLS // TN
    return pl.pallas_call(
        _in_proj_kernel,
        out_shape=(jax.ShapeDtypeStruct((m, MAIN_COLS + TN), F32), jax.ShapeDtypeStruct((m, GATE_COLS + TN), BF16)),
        grid=(m // tm, main_tiles + gate_tiles),
        in_specs=[
            pl.BlockSpec((tm, k), lambda i, j: (i, 0)),
            pl.BlockSpec((1, k), lambda i, j: (0, 0)),
            _wspec(k, TN, layer, lambda i, j: j),
        ],
        out_specs=(pl.BlockSpec((tm, TN), lambda i, j: (i, jnp.minimum(j, main_tiles))),
                   pl.BlockSpec((tm, TN), lambda i, j: (i, jnp.where(j < main_tiles, gate_tiles, j - main_tiles)))),
        scratch_shapes=[pltpu.VMEM((tm, k), BF16)],
        compiler_params=_params(("parallel", "arbitrary")),
        name="in_proj",
    )(x, g.reshape(1, k), w)


def _merge_out_kernel(ya_ref, yb_ref, g_ref, gates_ref, x_ref, wp_ref, wr_ref, wo_ref, o_ref, ybg_scr, m_scr):
    n = o_ref.shape[1]
    ybg_scr[...] = (yb_ref[...] * g_ref[...]).astype(BF16)
    for c in range(n // TN):
        sl = slice(c * TN, (c + 1) * TN)
        sr = slice(n + c * TN, n + (c + 1) * TN)
        a = jnp.dot(ya_ref[...], wp_ref[:, sl], preferred_element_type=F32)
        b = jnp.dot(ybg_scr[...], wr_ref[:, sl], preferred_element_type=F32)
        gate_a = jax.nn.sigmoid(gates_ref[:, sl].astype(F32))
        gate_b = jax.nn.sigmoid(gates_ref[:, sr].astype(F32))
        m_scr[:, sl] = (gate_a * a + gate_b * b).astype(BF16)
    for c in range(n // TN):
        sl = slice(c * TN, (c + 1) * TN)
        o_ref[:, sl] = x_ref[:, sl] + jnp.dot(m_scr[...], wo_ref[:, sl], preferred_element_type=F32)


def merge_out(ya, yb, g, gates, x, wp, wr, wo, layer):
    m, n = x.shape
    tm = TM_MERGE
    resident = lambda k_: pl.BlockSpec((None, k_, n), lambda i: (layer, 0, 0), pipeline_mode=pl.Buffered(1))
    return pl.pallas_call(
        _merge_out_kernel,
        out_shape=jax.ShapeDtypeStruct((m, n), F32),
        grid=(m // tm,),
        in_specs=[
            pl.BlockSpec((tm, POOL_WIDTH), lambda i: (i, 0)),
            pl.BlockSpec((tm, RW_WIDTH), lambda i: (i, 0)),
            pl.BlockSpec((tm, RW_WIDTH), lambda i: (i, 0)),
            pl.BlockSpec((tm, GATE_COLS), lambda i: (i, 0)),
            pl.BlockSpec((tm, n), lambda i: (i, 0)),
            resident(POOL_WIDTH), resident(RW_WIDTH), resident(n),
        ],
        out_specs=pl.BlockSpec((tm, n), lambda i: (i, 0)),
        scratch_shapes=[pltpu.VMEM((tm, RW_WIDTH), BF16), pltpu.VMEM((tm, n), BF16)],
        compiler_params=_params(("parallel",)),
        name="merge_out",
    )(ya, yb, g, gates, x, wp, wr, wo)


def _ffn_kernel(x_ref, g_ref, wg_ref, wu_ref, wd_ref, o_ref, h_scr):
    @pl.when(pl.program_id(1) == 0)
    def _():
        x = x_ref[...]
        h_scr[...] = _rms(x, g_ref[...]).astype(BF16)
        o_ref[...] = x

    h = h_scr[...]
    gate = jnp.dot(h, wg_ref[...], preferred_element_type=F32)
    up = jnp.dot(h, wu_ref[...], preferred_element_type=F32)
    act = (gate * jax.nn.sigmoid(gate) * up).astype(BF16)
    o_ref[...] += jnp.dot(act, wd_ref[...], preferred_element_type=F32)


def ffn(x, g, wg, wu, wd, layer):
    m, k = x.shape
    f = wg.shape[2]
    tm = TM_ROW
    return pl.pallas_call(
        _ffn_kernel,
        out_shape=jax.ShapeDtypeStruct((m, k), F32),
        grid=(m // tm, f // TN),
        in_specs=[
            pl.BlockSpec((tm, k), lambda i, j: (i, 0)),
            pl.BlockSpec((1, k), lambda i, j: (0, 0)),
            _wspec(k, TN, layer, lambda i, j: j),
            _wspec(k, TN, layer, lambda i, j: j),
            pl.BlockSpec((None, TN, k), lambda i, j: (layer, j, 0)),
        ],
        out_specs=pl.BlockSpec((tm, k), lambda i, j: (i, 0)),
        scratch_shapes=[pltpu.VMEM((tm, k), BF16)],
        compiler_params=_params(("parallel", "arbitrary")),
        name="ffn",
    )(x, g.reshape(1, k), wg, wu, wd)


def _ple_kernel(x_ref, g_ref, wg_ref, pp_ref, ps_ref, wp_ref, o_ref, h_scr, *, prompt_tiles):
    n = o_ref.shape[1]
    h_scr[...] = _rms(x_ref[...], g_ref[...]).astype(BF16)
    pb = jnp.where(pl.program_id(0) < prompt_tiles, pp_ref[...], ps_ref[...]).astype(BF16)
    for c in range(n // TN):
        sl = slice(c * TN, (c + 1) * TN)
        gate = jnp.dot(h_scr[...], wg_ref[:, sl], preferred_element_type=F32)
        emb = jnp.dot(pb, wp_ref[:, sl], preferred_element_type=F32)
        o_ref[:, sl] = x_ref[:, sl] + jax.nn.sigmoid(gate) * emb


def ple_update(x, g, wg, p_prompt, p_sample, wp, layer):
    m, k = x.shape
    n = wg.shape[2]
    kp = p_prompt.shape[2]
    tm = TM_PLE
    prompt_tiles = p_prompt.shape[1] // tm
    sample_tiles = p_sample.shape[1] // tm
    return pl.pallas_call(
        functools.partial(_ple_kernel, prompt_tiles=prompt_tiles),
        out_shape=jax.ShapeDtypeStruct((m, n), F32),
        grid=(m // tm,),
        in_specs=[
            pl.BlockSpec((tm, k), lambda i: (i, 0)),
            pl.BlockSpec((1, k), lambda i: (0, 0)),
            _wspec(k, n, layer, lambda i: 0),
            pl.BlockSpec((None, tm, kp), lambda i: (layer, jnp.minimum(i, prompt_tiles - 1), 0)),
            pl.BlockSpec((None, tm, kp), lambda i: (layer, jnp.clip(i - prompt_tiles, 0, sample_tiles - 1), 0)),
            _wspec(kp, n, layer, lambda i: 0),
        ],
        out_specs=pl.BlockSpec((tm, n), lambda i: (i, 0)),
        scratch_shapes=[pltpu.VMEM((tm, k), BF16)],
        compiler_params=_params(("parallel",)),
        name="ple_update",
    )(x, g.reshape(1, k), wg, p_prompt, p_sample, wp)


def _final_norm_kernel(x_ref, g_ref, o_ref):
    o_ref[...] = _rms(x_ref[...], g_ref[...])


def final_norm(x, g, *, row0, rows):
    k = x.shape[1]
    tm = 512
    rb0 = row0 // tm
    return pl.pallas_call(
        _final_norm_kernel,
        out_shape=jax.ShapeDtypeStruct((rows, k), F32),
        grid=(rows // tm,),
        in_specs=[pl.BlockSpec((tm, k), lambda i: (rb0 + i, 0)), pl.BlockSpec((1, k), lambda i: (0, 0))],
        out_specs=pl.BlockSpec((tm, k), lambda i: (i, 0)),
        compiler_params=_params(("parallel",)),
        name="final_norm",
    )(x, g.reshape(1, k))


def _shift_mix(cur, carry_row, mu):
    rolled = pltpu.roll(cur, 1, axis=0)
    row = lax.broadcasted_iota(jnp.int32, cur.shape, 0)
    prev = jnp.where(row == 0, carry_row, rolled)
    return cur + (prev - cur) * mu


def _softplus(x):
    return jnp.maximum(x, 0.0) + jnp.log1p(jnp.exp(-jnp.abs(x)))


def _mm(a, b):
    return jnp.dot(a.astype(BF16), b.astype(BF16), preferred_element_type=F32)


def _mm_nt(a, b):
    return lax.dot_general(a.astype(BF16), b.astype(BF16), (((1,), (1,)), ((), ())),
                           preferred_element_type=F32)


def _mm_tn(a, b):
    return lax.dot_general(a.astype(BF16), b.astype(BF16), (((0,), (0,)), ((), ())),
                           preferred_element_type=F32)


def _split2(x):
    hi = x.astype(BF16)
    lo = (x - hi.astype(F32)).astype(BF16)
    return hi, lo


def _mm_hilo(a, b):
    ah, al = _split2(a)
    bh, bl = _split2(b)
    d = lambda x, y: jnp.dot(x, y, preferred_element_type=F32)
    return d(ah, bh) + d(ah, bl) + d(al, bh)


def _segsum(x, seg):
    xb = x.astype(BF16)
    parts = [jnp.dot(xb[:, c * LANES:(c + 1) * LANES], seg, preferred_element_type=F32) for c in range(PAIRS)]
    return jnp.concatenate(parts, axis=-1)


def _stack2(x, m0):
    return jnp.concatenate([jnp.where(m0, x, 0.0), jnp.where(m0, 0.0, x)], axis=0)


def _seq_kernel(r_ref, k_ref, v_ref, l_ref, u_ref, sp_ref, spl_ref, past_ref, s0_ref,
                mu_ref, mul_ref, w0_ref, a0_ref, kkw_ref, ka_ref, wd_ref, wa_ref, wg_ref,
                pw_ref, ps_ref, gain_ref, bias_ref, rk_ref, seg_ref, tri_ref,
                ya_ref, yb_ref, g_ref, ns_ref, nsl_ref, np_ref, s_ref,
                carry, carry_l, carry_p, *, prompt_tiles, tiles_p, tiles_s):
    c = CHUNK
    i = pl.program_id(0)
    in_prompt = i < prompt_tiles
    t = jnp.where(in_prompt, i % tiles_p, (i - prompt_tiles) % tiles_s)
    pos0 = jnp.where(in_prompt, 0, PAST_LEN) + t * c

    @pl.when(t == 0)
    def _():
        carry[0:3, :] = sp_ref[0]
        carry_l[0:1, :] = spl_ref[0]
        carry_p[...] = past_ref[0]
        s_ref[...] = s0_ref[...]

    u = u_ref[...]
    ext = jnp.concatenate([carry_p[...], u], axis=0)
    carry_p[...] = ext[c:c + POOL_PAD]
    np_ref[0] = ext[c:c + POOL_PAD]
    pos = pos0 + lax.broadcasted_iota(jnp.int32, (c, 1), 0)
    outs = []
    for gi, win in enumerate(POOL_WINDOWS):
        sl = slice(gi * POOL_GC, (gi + 1) * POOL_GC)
        sw = ext[:, sl]
        sh = 1
        while sh < win:
            sw = sw + pltpu.roll(sw, sh, axis=0)
            sh *= 2
        cnt = jnp.minimum(pos + 1, win).astype(F32)
        dlt = sw[POOL_PAD:] / cnt - u[:, sl]
        outs.append(jnp.dot(dlt.astype(BF16), pw_ref[gi], preferred_element_type=F32))
    ya_ref[...] = (jnp.concatenate(outs, axis=-1) * ps_ref[...]).astype(BF16)

    r_in = r_ref[...]
    k_in = k_ref[...]
    v_in = v_ref[...]
    lo_in = l_ref[...]
    r = _shift_mix(r_in, carry[0:1, :], mu_ref[0:1, :])
    xk = _shift_mix(k_in, carry[1:2, :], mu_ref[1:2, :])
    v = _shift_mix(v_in, carry[2:3, :], mu_ref[2:3, :])
    xl = _shift_mix(lo_in, carry_l[0:1, :], mul_ref[...])
    last = jnp.concatenate([r_in[c - 1:c, :], k_in[c - 1:c, :], v_in[c - 1:c, :]], axis=0)
    carry[0:3, :] = last
    carry_l[0:1, :] = lo_in[c - 1:c, :]
    ns_ref[0] = last
    nsl_ref[0] = lo_in[c - 1:c, :]

    seg = seg_ref[...]
    xda = xl[:, 0:LANES]
    xg = xl[:, LANES:LANES + LORA_GATE_PAD]
    zd = jnp.dot(jnp.tanh(xda).astype(BF16), wd_ref[...], preferred_element_type=F32)
    logw = -jnp.exp(-_softplus(-(w0_ref[...] + zd)) - 0.5)
    za = jnp.dot(xda.astype(BF16), wa_ref[...], preferred_element_type=F32)
    a = jax.nn.sigmoid(a0_ref[...] + za)
    g_ref[...] = jnp.dot(jax.nn.sigmoid(xg).astype(BF16), wg_ref[...], preferred_element_type=F32)
    kkr = xk * kkw_ref[...]
    kk = kkr / jnp.maximum(jnp.sqrt(_segsum(kkr * kkr, seg)), 1e-12)
    k = xk * (1.0 + (a - 1.0) * ka_ref[...])
    bb = kk * a

    hi, lo = _split2(logw)
    tri = tri_ref[...]
    linc = jnp.dot(tri, hi, preferred_element_type=F32) + jnp.dot(tri, lo, preferred_element_type=F32)
    lexc = linc - logw
    ltot = linc[c - 1:c, :]
    e_ninc = jnp.exp(-linc)
    kt = kk * jnp.exp(lexc)
    rt = r * jnp.exp(linc)
    bh = bb * e_ninc
    kh = k * e_ninc
    e_rem = jnp.exp(ltot - linc)
    bbar = bb * e_rem
    kbar = k * e_rem
    gtot = jnp.exp(ltot)

    ri = lax.broadcasted_iota(jnp.int32, (2 * c, 2 * c), 0)
    ci = lax.broadcasted_iota(jnp.int32, (2 * c, 2 * c), 1)
    strict = (ci % c) < (ri % c)
    incl = (ci % c) <= (ri % c)
    blk = (ri // SUB) == (ci // SUB)
    eye = ri == ci
    m0 = lax.broadcasted_iota(jnp.int32, (c, LANES), 1) < RW_HEAD

    pairs = range(PAIRS)
    lanes = [slice(p * LANES, (p + 1) * LANES) for p in pairs]
    each = lambda fn, *cols: [fn(*args) for args in zip(*cols)]
    stacked = lambda x: [_stack2(x[:, sl], m0) for sl in lanes]
    kt2, rt2, bh2, kh2, bbar2, kbar2, v2 = (stacked(x) for x in (kt, rt, bh, kh, bbar, kbar, v))

    aa = each(lambda a_, b_, c_, d_: _mm_nt(jnp.concatenate([a_, b_], axis=0), jnp.concatenate([c_, d_], axis=0)),
              kt2, rt2, bh2, kh2)
    lp = [jnp.where(strict, x[0:2 * c, 0:2 * c], 0.0) for x in aa]
    akp = [jnp.where(strict, x[0:2 * c, 2 * c:4 * c], 0.0) for x in aa]
    arb = [jnp.where(incl, x[2 * c:4 * c, 0:2 * c], 0.0) for x in aa]
    ark = [jnp.where(incl, x[2 * c:4 * c, 2 * c:4 * c], 0.0) for x in aa]

    ld = [jnp.where(blk, x, 0.0) for x in lp]
    lo_ = each(lambda x, y_: x - y_, lp, ld)
    p2 = each(_mm, ld, ld)
    p4 = each(_mm, p2, p2)
    p8 = each(_mm, p4, p4)
    e1 = each(lambda a_, b_, m: a_ - b_ - m, p2, ld, each(_mm, ld, p2))
    e2 = each(lambda a_, b_, m: a_ + b_ + m, e1, p4, each(_mm, e1, p4))
    dm = each(lambda a_, b_, m: a_ + b_ + m, e2, p8, each(_mm, e2, p8))
    n1 = each(lambda a_, m: a_ + m, lo_, each(_mm, dm, lo_))
    n2 = each(_mm, n1, n1)
    f = each(lambda a_, b_, m: a_ - b_ - m, n2, n1, each(_mm, n1, n2))
    tm = each(lambda a_, b_, m: a_ + b_ + m, f, dm, each(_mm, f, dm))

    akv = each(_mm, akp, v2)
    uu = each(lambda a_, b_: jnp.concatenate([a_, b_], axis=1), kt2, akv)
    pq = each(lambda a_, m: a_ + m, uu, each(_mm, tm, uu))
    rbpq = each(_mm, arb, pq)
    ry = each(lambda a_, m: a_ - m[:, 0:LANES], rt2, rbpq)
    y0 = each(lambda m, n_: m - n_[:, LANES:2 * LANES], each(_mm, ark, v2), rbpq)
    btpq = each(_mm_tn, bbar2, pq)
    g2 = [jnp.where(eye, gtot[:, sl], 0.0) - m[:, 0:LANES] for sl, m in zip(lanes, btpq)]
    h2 = each(lambda m, n_: m - n_[:, LANES:2 * LANES], each(_mm_tn, kbar2, v2), btpq)

    st = [s_ref[0, p] for p in pairs]
    y2 = each(lambda a_, b_, o: _mm(a_, b_) + o, ry, st, y0)
    s_new = each(lambda a_, b_, o: _mm_hilo(a_, b_) + o, g2, st, h2)
    for p in pairs:
        s_ref[0, p] = s_new[p]
    y = jnp.concatenate([x[0:c] + x[c:2 * c] for x in y2], axis=-1)

    inv_n = 1.0 / RW_HEAD
    mean = _segsum(y, seg) * inv_n
    yc = y - mean
    var = _segsum(yc * yc, seg) * inv_n
    yn = yc * lax.rsqrt(var + GN_EPS) * gain_ref[...] + bias_ref[...]
    bonus = _segsum(r * k * rk_ref[...], seg)
    yb_ref[...] = yn + bonus * v


def seq_mixers(proj, shift_rkv, shift_lora, pool_past, s0_blk, lw, *, prompt_batch, prompt_len, sample_len):
    c = CHUNK
    n_seq = shift_rkv.shape[0]
    tiles_p = prompt_len // c
    tiles_s = sample_len // c
    prompt_tiles = prompt_batch * tiles_p
    m = proj.shape[0]
    n_tiles = m // c
    seq = lambda i: jnp.where(i < prompt_tiles, i // tiles_p, prompt_batch + (i - prompt_tiles) // tiles_s)
    col = lambda width, cb: pl.BlockSpec((c, width), lambda i: (i, cb))
    per_seq = lambda *shape: pl.BlockSpec((1,) + shape, lambda i: (seq(i),) + tuple(0 for _ in shape))
    full = lambda *shape: pl.BlockSpec(shape, lambda i: tuple(0 for _ in shape))
    tri = (jnp.arange(c)[:, None] >= jnp.arange(c)[None, :]).astype(BF16)
    row = lambda v: v.reshape(1, -1)
    return pl.pallas_call(
        functools.partial(_seq_kernel, prompt_tiles=prompt_tiles, tiles_p=tiles_p, tiles_s=tiles_s),
        out_shape=(jax.ShapeDtypeStruct((m, POOL_WIDTH), BF16),
                   jax.ShapeDtypeStruct((m, RW_WIDTH), F32),
                   jax.ShapeDtypeStruct((m, RW_WIDTH), F32),
                   jax.ShapeDtypeStruct((n_seq, 3, RW_WIDTH), F32),
                   jax.ShapeDtypeStruct((n_seq, 1, LORA_PAD), F32),
                   jax.ShapeDtypeStruct((n_seq, POOL_PAD, POOL_WIDTH), F32),
                   jax.ShapeDtypeStruct((n_seq, PAIRS, LANES, LANES), F32)),
        grid=(n_tiles,),
        in_specs=[
            col(RW_WIDTH, 0), col(RW_WIDTH, 1), col(RW_WIDTH, 2), col(LORA_PAD, COL_LORA // LORA_PAD),
            col(POOL_WIDTH, COL_POOL // POOL_WIDTH),
            per_seq(3, RW_WIDTH), per_seq(1, LORA_PAD), per_seq(POOL_PAD, POOL_WIDTH), per_seq(PAIRS, LANES, LANES),
            full(3, RW_WIDTH), full(1, LORA_PAD),
            full(1, RW_WIDTH), full(1, RW_WIDTH), full(1, RW_WIDTH), full(1, RW_WIDTH),
            full(LANES, RW_WIDTH), full(LANES, RW_WIDTH), full(LORA_GATE_PAD, RW_WIDTH),
            full(POOL_GROUPS, POOL_GC, POOL_GC), full(1, POOL_WIDTH),
            full(1, RW_WIDTH), full(1, RW_WIDTH), full(1, RW_WIDTH),
            full(LANES, LANES), full(c, c),
        ],
        out_specs=(col(POOL_WIDTH, 0), col(RW_WIDTH, 0), col(RW_WIDTH, 0),
                   per_seq(3, RW_WIDTH), per_seq(1, LORA_PAD), per_seq(POOL_PAD, POOL_WIDTH),
                   per_seq(PAIRS, LANES, LANES)),
        scratch_shapes=[pltpu.VMEM((8, RW_WIDTH), F32), pltpu.VMEM((8, LORA_PAD), F32),
                        pltpu.VMEM((POOL_PAD, POOL_WIDTH), F32)],
        compiler_params=_params(("arbitrary",)),
        name="seq_mixers",
    )(proj, proj, proj, proj, proj, shift_rkv, shift_lora, pool_past, s0_blk,
      lw["mu_rkv"], lw["mu_lora"], lw["w0"], lw["a0"], lw["k_k"], lw["k_a"], lw["wd"], lw["wa"], lw["wg"],
      lw["pool_w"], row(lw["pool_scale"]), row(lw["gn_gain"]), row(lw["gn_bias"]), row(lw["r_k"]), lw["seg"], tri)


def _state_to_blocks(s0):
    b = s0.shape[0]
    st = jnp.swapaxes(s0, -1, -2).reshape(b, PAIRS, 2, RW_HEAD, RW_HEAD)
    z = jnp.zeros_like(st[:, :, 0])
    top = jnp.concatenate([st[:, :, 0], z], axis=-1)
    bottom = jnp.concatenate([z, st[:, :, 1]], axis=-1)
    return jnp.concatenate([top, bottom], axis=-2)


def _blocks_to_state(sb):
    b = sb.shape[0]
    d = jnp.stack([sb[:, :, :RW_HEAD, :RW_HEAD], sb[:, :, RW_HEAD:, RW_HEAD:]], axis=2)
    return jnp.swapaxes(d.reshape(b, RW_HEADS, RW_HEAD, RW_HEAD), -1, -2)


def _stacked_weights(w_in, pool_w, w_decay_up, w_aaa_up, w_gate_up, proj_pool, proj_rwkv, w_out,
                     w_ffn_gate, w_ffn_up, w_ffn_down, w_ple_gate, w_ple_proj):
    zpad = lambda rows: jnp.zeros((DEPTH, rows, RW_WIDTH), F32)
    return dict(
        w_in=w_in_prep(w_in), pool_w=pool_w.astype(BF16),
        wd=jnp.concatenate([w_decay_up, zpad(LANES - DECAY_LORA)], axis=1).astype(BF16),
        wa=jnp.concatenate([zpad(DECAY_LORA), w_aaa_up], axis=1).astype(BF16),
        wg=jnp.concatenate([w_gate_up, zpad(LORA_GATE_PAD - GATE_LORA)], axis=1).astype(BF16),
        proj_pool=proj_pool.astype(BF16), proj_rwkv=proj_rwkv.astype(BF16), w_out=w_out.astype(BF16),
        w_ffn_gate=w_ffn_gate.astype(BF16), w_ffn_up=w_ffn_up.astype(BF16), w_ffn_down=w_ffn_down.astype(BF16),
        w_ple_gate=w_ple_gate.astype(BF16), w_ple_proj=w_ple_proj.astype(BF16))


def kernel(x_prompt, x_sample, state_shift, state_pool, state_wkv, p_prompt, p_sample, norm_mix, w_in, mu_shift, pool_w, pool_scale, w0, w_decay_up, a0, w_aaa_up, w_gate_up, k_k, k_a, r_k, gn_gain, gn_bias, proj_pool, proj_rwkv, w_out, norm_ffn, w_ffn_gate, w_ffn_up, w_ffn_down, norm_ple, w_ple_gate, w_ple_proj, norm_final):
    bp, tp, _ = x_prompt.shape
    bs, ts, _ = x_sample.shape
    mp = bp * tp
    ms = bs * ts
    x = jnp.concatenate([x_prompt.reshape(mp, D_MODEL), x_sample.reshape(ms, D_MODEL)], axis=0)
    pp_tok = p_prompt.reshape(DEPTH, mp, PLE_DIM)
    ps_tok = p_sample.reshape(DEPTH, ms, PLE_DIM)
    sw = _stacked_weights(w_in, pool_w, w_decay_up, w_aaa_up, w_gate_up, proj_pool, proj_rwkv, w_out,
                          w_ffn_gate, w_ffn_up, w_ffn_down, w_ple_gate, w_ple_proj)
    lane = jnp.arange(LANES) // RW_HEAD
    seg = (lane[:, None] == lane[None, :]).astype(BF16)
    lead = lambda a: jnp.concatenate([jnp.zeros((DEPTH, bp) + a.shape[2:], a.dtype), a], axis=1)
    shift_all = lead(state_shift)
    shift_rkv = shift_all[:, :, :3 * RW_WIDTH].reshape(DEPTH, bp + bs, 3, RW_WIDTH)
    shift_lora = jnp.pad(shift_all[:, :, 3 * RW_WIDTH:], ((0, 0), (0, 0), (0, LORA_PAD - LORA)))
    shift_lora = shift_lora.reshape(DEPTH, bp + bs, 1, LORA_PAD)
    pool_all = jnp.pad(lead(state_pool), ((0, 0), (0, 0), (POOL_PAD - POOL_PAST, 0), (0, 0)))
    wkv_all = lead(state_wkv)
    row = lambda v: v.reshape(1, -1)
    new_shift, new_pool, new_wkv = [], [], []
    for i in range(DEPTH):
        mu = mu_shift[i]
        lw = dict(
            mu_rkv=mu[:3 * RW_WIDTH].reshape(3, RW_WIDTH),
            mu_lora=jnp.pad(mu[3 * RW_WIDTH:], (0, LORA_PAD - LORA)).reshape(1, LORA_PAD),
            pool_w=sw["pool_w"][i], pool_scale=pool_scale[i],
            w0=row(w0[i]), a0=row(a0[i]), k_k=row(k_k[i]), k_a=row(k_a[i]),
            wd=sw["wd"][i], wa=sw["wa"][i], wg=sw["wg"][i], seg=seg,
            r_k=r_k[i].reshape(RW_WIDTH), gn_gain=gn_gain[i], gn_bias=gn_bias[i])
        proj, gates = in_proj(x, norm_mix[i], sw["w_in"], i)
        ya, yb, g, ns_rkv, ns_lora, pool_rows, s_blk = seq_mixers(
            proj, shift_rkv[i], shift_lora[i], pool_all[i], _state_to_blocks(wkv_all[i]), lw,
            prompt_batch=bp, prompt_len=tp, sample_len=ts)
        new_shift.append(jnp.concatenate([ns_rkv.reshape(bp + bs, 3 * RW_WIDTH), ns_lora[:, 0, :LORA]], axis=1))
        new_pool.append(pool_rows[:, POOL_PAD - POOL_PAST:])
        new_wkv.append(_blocks_to_state(s_blk))
        x = merge_out(ya, yb, g, gates, x, sw["proj_pool"], sw["proj_rwkv"], sw["w_out"], i)
        x = ffn(x, norm_ffn[i], sw["w_ffn_gate"], sw["w_ffn_up"], sw["w_ffn_down"], i)
        x = ple_update(x, norm_ple[i], sw["w_ple_gate"], pp_tok, ps_tok, sw["w_ple_proj"], i)
    y_prompt = final_norm(x, norm_final, row0=0, rows=mp).reshape(bp, tp, D_MODEL)
    y_sample = final_norm(x, norm_final, row0=mp, rows=ms).reshape(bs, ts, D_MODEL)
    new_shift, new_pool, new_wkv = jnp.stack(new_shift), jnp.stack(new_pool), jnp.stack(new_wkv)
    return (y_prompt, y_sample, new_shift[:, :bp], new_pool[:, :bp], new_wkv[:, :bp],
            new_shift[:, bp:], new_pool[:, bp:], new_wkv[:, bp:])
```

```python
import functools

import jax
import jax.numpy as jnp
from jax import lax
from jax.experimental import pallas as pl
from jax.experimental.pallas import tpu as pltpu

F32 = jnp.float32
BF16 = jnp.bfloat16

D_MODEL = 2048
DEPTH = 4
PAST_LEN = 1024
PLE_DIM = 256
POOL_WINDOWS = (2, 4, 8, 16)
POOL_GROUPS = 4
POOL_WIDTH = D_MODEL // 2
POOL_GC = POOL_WIDTH // POOL_GROUPS
POOL_PAST = max(POOL_WINDOWS) - 1
RW_WIDTH = D_MODEL // 2
RW_HEAD = 64
RW_HEADS = RW_WIDTH // RW_HEAD
DECAY_LORA = 64
AAA_LORA = 64
GATE_LORA = 160
LORA = DECAY_LORA + AAA_LORA + GATE_LORA
RW_PROJ = 3 * RW_WIDTH + LORA
D_FF = 5632
NORM_EPS = 1e-6
GN_EPS = 64e-5

LANES = 128
POOL_PAD = 16
LORA_PAD = 512
LORA_GATE_PAD = 256
PAIRS = RW_WIDTH // LANES
CHUNK = 64
SUB = 16
COL_RKV = 0
COL_POOL = 3 * RW_WIDTH
COL_LORA = COL_POOL + POOL_WIDTH
MAIN_COLS = COL_LORA + LORA_PAD
GATE_COLS = 2 * D_MODEL

TN = 512
TM_WIDE = 1536
TM_ROW = 768
TM_MERGE = 384
TM_PLE = 512
VMEM_LIMIT = 56 * 1024 * 1024


def _params(sem):
    return pltpu.CompilerParams(dimension_semantics=sem, vmem_limit_bytes=VMEM_LIMIT)


def _rms(x, g):
    ms = jnp.mean(x * x, axis=-1, keepdims=True)
    return x * lax.rsqrt(ms + NORM_EPS) * g


def _wspec(k, n, layer, col_of):
    return pl.BlockSpec((None, k, n), lambda *idx: (layer, 0, col_of(*idx)))


def _in_proj_kernel(x_ref, g_ref, wt_ref, main_ref, gates_ref, h_scr):
    @pl.when(pl.program_id(1) == 0)
    def _():
        h_scr[...] = _rms(x_ref[...], g_ref[...]).astype(BF16)

    acc = lax.dot_general(h_scr[...], wt_ref[...], (((1,), (1,)), ((), ())), preferred_element_type=F32)
    main_ref[...] = acc
    gates_ref[...] = acc.astype(BF16)


def in_proj(x, g, wt, layer):
    m, k = x.shape
    tm = TM_WIDE
    main_tiles = MAIN_COLS // TN
    gate_tiles = GATE_COLS // TN
    return pl.pallas_call(
        _in_proj_kernel,
        out_shape=(jax.ShapeDtypeStruct((m, MAIN_COLS + TN), F32), jax.ShapeDtypeStruct((m, GATE_COLS + TN), BF16)),
        grid=(m // tm, main_tiles + gate_tiles),
        in_specs=[
            pl.BlockSpec((tm, k), lambda i, j: (i, 0)),
            pl.BlockSpec((1, k), lambda i, j: (0, 0)),
            pl.BlockSpec((None, TN, k), lambda i, j: (layer, j, 0)),
        ],
        out_specs=(pl.BlockSpec((tm, TN), lambda i, j: (i, jnp.minimum(j, main_tiles))),
                   pl.BlockSpec((tm, TN), lambda i, j: (i, jnp.where(j < main_tiles, gate_tiles, j - main_tiles)))),
        scratch_shapes=[pltpu.VMEM((tm, k), BF16)],
        compiler_params=_params(("parallel", "arbitrary")),
        name="in_proj",
    )(x, g.reshape(1, k), wt)


def _merge_out_kernel(ya_ref, yb_ref, g_ref, gates_ref, x_ref, wp_ref, wr_ref, wo_ref, o_ref, ybg_scr, m_scr):
    n = o_ref.shape[1]
    ybg_scr[...] = (yb_ref[...] * g_ref[...]).astype(BF16)
    for c in range(n // TN):
        sl = slice(c * TN, (c + 1) * TN)
        sr = slice(n + c * TN, n + (c + 1) * TN)
        a = jnp.dot(ya_ref[...], wp_ref[:, sl], preferred_element_type=F32)
        b = jnp.dot(ybg_scr[...], wr_ref[:, sl], preferred_element_type=F32)
        gate_a = jax.nn.sigmoid(gates_ref[:, sl].astype(F32))
        gate_b = jax.nn.sigmoid(gates_ref[:, sr].astype(F32))
        m_scr[:, sl] = (gate_a * a + gate_b * b).astype(BF16)
    for c in range(n // TN):
        sl = slice(c * TN, (c + 1) * TN)
        o_ref[:, sl] = x_ref[:, sl] + jnp.dot(m_scr[...], wo_ref[:, sl], preferred_element_type=F32)


def merge_out(ya, yb, g, gates, x, wp, wr, wo, layer):
    m, n = x.shape
    tm = TM_MERGE
    resident = lambda k_: pl.BlockSpec((None, k_, n), lambda i: (layer, 0, 0), pipeline_mode=pl.Buffered(1))
    return pl.pallas_call(
        _merge_out_kernel,
        out_shape=jax.ShapeDtypeStruct((m, n), F32),
        grid=(m // tm,),
        in_specs=[
            pl.BlockSpec((tm, POOL_WIDTH), lambda i: (i, 0)),
            pl.BlockSpec((tm, RW_WIDTH), lambda i: (i, 0)),
            pl.BlockSpec((tm, RW_WIDTH), lambda i: (i, 0)),
            pl.BlockSpec((tm, GATE_COLS), lambda i: (i, 0)),
            pl.BlockSpec((tm, n), lambda i: (i, 0)),
            resident(POOL_WIDTH), resident(RW_WIDTH), resident(n),
        ],
        out_specs=pl.BlockSpec((tm, n), lambda i: (i, 0)),
        scratch_shapes=[pltpu.VMEM((tm, RW_WIDTH), BF16), pltpu.VMEM((tm, n), BF16)],
        compiler_params=_params(("parallel",)),
        name="merge_out",
    )(ya, yb, g, gates, x, wp, wr, wo)


def _ffn_kernel(x_ref, g_ref, wg_ref, wu_ref, wd_ref, o_ref, h_scr):
    @pl.when(pl.program_id(1) == 0)
    def _():
        x = x_ref[...]
        h_scr[...] = _rms(x, g_ref[...]).astype(BF16)
        o_ref[...] = x

    h = h_scr[...]
    gate = jnp.dot(h, wg_ref[...], preferred_element_type=F32)
    up = jnp.dot(h, wu_ref[...], preferred_element_type=F32)
    act = (gate * jax.nn.sigmoid(gate) * up).astype(BF16)
    o_ref[...] += jnp.dot(act, wd_ref[...], preferred_element_type=F32)


def ffn(x, g, wg, wu, wd, layer):
    m, k = x.shape
    f = wg.shape[2]
    tm = TM_ROW
    return pl.pallas_call(
        _ffn_kernel,
        out_shape=jax.ShapeDtypeStruct((m, k), F32),
        grid=(m // tm, f // TN),
        in_specs=[
            pl.BlockSpec((tm, k), lambda i, j: (i, 0)),
            pl.BlockSpec((1, k), lambda i, j: (0, 0)),
            _wspec(k, TN, layer, lambda i, j: j),
            _wspec(k, TN, layer, lambda i, j: j),
            pl.BlockSpec((None, TN, k), lambda i, j: (layer, j, 0)),
        ],
        out_specs=pl.BlockSpec((tm, k), lambda i, j: (i, 0)),
        scratch_shapes=[pltpu.VMEM((tm, k), BF16)],
        compiler_params=_params(("parallel", "arbitrary")),
        name="ffn",
    )(x, g.reshape(1, k), wg, wu, wd)


def _ple_kernel(x_ref, g_ref, wg_ref, pp_ref, ps_ref, wp_ref, o_ref, h_scr, *, prompt_tiles):
    n = o_ref.shape[1]
    h_scr[...] = _rms(x_ref[...], g_ref[...]).astype(BF16)
    pb = jnp.where(pl.program_id(0) < prompt_tiles, pp_ref[...], ps_ref[...]).astype(BF16)
    for c in range(n // TN):
        sl = slice(c * TN, (c + 1) * TN)
        gate = jnp.dot(h_scr[...], wg_ref[:, sl], preferred_element_type=F32)
        emb = jnp.dot(pb, wp_ref[:, sl], preferred_element_type=F32)
        o_ref[:, sl] = x_ref[:, sl] + jax.nn.sigmoid(gate) * emb


def ple_update(x, g, wg, p_prompt, p_sample, wp, layer):
    m, k = x.shape
    n = wg.shape[2]
    kp = p_prompt.shape[2]
    tm = TM_PLE
    prompt_tiles = p_prompt.shape[1] // tm
    sample_tiles = p_sample.shape[1] // tm
    return pl.pallas_call(
        functools.partial(_ple_kernel, prompt_tiles=prompt_tiles),
        out_shape=jax.ShapeDtypeStruct((m, n), F32),
        grid=(m // tm,),
        in_specs=[
            pl.BlockSpec((tm, k), lambda i: (i, 0)),
            pl.BlockSpec((1, k), lambda i: (0, 0)),
            _wspec(k, n, layer, lambda i: 0),
            pl.BlockSpec((None, tm, kp), lambda i: (layer, jnp.minimum(i, prompt_tiles - 1), 0)),
            pl.BlockSpec((None, tm, kp), lambda i: (layer, jnp.clip(i - prompt_tiles, 0, sample_tiles - 1), 0)),
            _wspec(kp, n, layer, lambda i: 0),
        ],
        out_specs=pl.BlockSpec((tm, n), lambda i: (i, 0)),
        scratch_shapes=[pltpu.VMEM((tm, k), BF16)],
        compiler_params=_params(("parallel",)),
        name="ple_update",
    )(x, g.reshape(1, k), wg, p_prompt, p_sample, wp)


def _final_norm_kernel(x_ref, g_ref, o_ref):
    o_ref[...] = _rms(x_ref[...], g_ref[...])


def final_norm(x, g, *, row0, rows):
    k = x.shape[1]
    tm = 512
    rb0 = row0 // tm
    return pl.pallas_call(
        _final_norm_kernel,
        out_shape=jax.ShapeDtypeStruct((rows, k), F32),
        grid=(rows // tm,),
        in_specs=[pl.BlockSpec((tm, k), lambda i: (rb0 + i, 0)), pl.BlockSpec((1, k), lambda i: (0, 0))],
        out_specs=pl.BlockSpec((tm, k), lambda i: (i, 0)),
        compiler_params=_params(("parallel",)),
        name="final_norm",
    )(x, g.reshape(1, k))


def _shift_mix(cur, carry_row, mu):
    rolled = pltpu.roll(cur, 1, axis=0)
    row = lax.broadcasted_iota(jnp.int32, cur.shape, 0)
    prev = jnp.where(row == 0, carry_row, rolled)
    return cur + (prev - cur) * mu


def _softplus(x):
    return jnp.maximum(x, 0.0) + jnp.log1p(jnp.exp(-jnp.abs(x)))


def _mm(a, b):
    return jnp.dot(a.astype(BF16), b.astype(BF16), preferred_element_type=F32)


def _mm_nt(a, b):
    return lax.dot_general(a.astype(BF16), b.astype(BF16), (((1,), (1,)), ((), ())),
                           preferred_element_type=F32)


def _mm_tn(a, b):
    return lax.dot_general(a.astype(BF16), b.astype(BF16), (((0,), (0,)), ((), ())),
                           preferred_element_type=F32)


def _split2(x):
    hi = x.astype(BF16)
    lo = (x - hi.astype(F32)).astype(BF16)
    return hi, lo


def _mm_hilo(a, b):
    ah, al = _split2(a)
    bh, bl = _split2(b)
    d = lambda x, y: jnp.dot(x, y, preferred_element_type=F32)
    return d(ah, bh) + d(ah, bl) + d(al, bh)


def _segsum(x, seg):
    xb = x.astype(BF16)
    parts = [jnp.dot(xb[:, c * LANES:(c + 1) * LANES], seg, preferred_element_type=F32) for c in range(PAIRS)]
    return jnp.concatenate(parts, axis=-1)


def _stack2(x, m0):
    return jnp.concatenate([jnp.where(m0, x, 0.0), jnp.where(m0, 0.0, x)], axis=0)


def _seq_kernel(r_ref, k_ref, v_ref, l_ref, u_ref, sp_ref, spl_ref, past_ref, s0_ref,
                mu_ref, mul_ref, w0_ref, a0_ref, kkw_ref, ka_ref, wd_ref, wa_ref, wg_ref,
                pw_ref, ps_ref, gain_ref, bias_ref, rk_ref, seg_ref, tri_ref,
                ya_ref, yb_ref, g_ref, ns_ref, nsl_ref, np_ref, s_ref,
                carry, carry_l, carry_p, *, prompt_tiles, tiles_p, tiles_s):
    c = CHUNK
    i = pl.program_id(0)
    in_prompt = i < prompt_tiles
    t = jnp.where(in_prompt, i % tiles_p, (i - prompt_tiles) % tiles_s)
    pos0 = jnp.where(in_prompt, 0, PAST_LEN) + t * c

    @pl.when(t == 0)
    def _():
        carry[0:3, :] = sp_ref[0]
        carry_l[0:1, :] = spl_ref[0]
        carry_p[...] = past_ref[0]
        s_ref[...] = s0_ref[...]

    u = u_ref[...]
    ext = jnp.concatenate([carry_p[...], u], axis=0)
    carry_p[...] = ext[c:c + POOL_PAD]
    np_ref[0] = ext[c:c + POOL_PAD]
    pos = pos0 + lax.broadcasted_iota(jnp.int32, (c, 1), 0)
    outs = []
    for gi, win in enumerate(POOL_WINDOWS):
        sl = slice(gi * POOL_GC, (gi + 1) * POOL_GC)
        sw = ext[:, sl]
        sh = 1
        while sh < win:
            sw = sw + pltpu.roll(sw, sh, axis=0)
            sh *= 2
        cnt = jnp.minimum(pos + 1, win).astype(F32)
        dlt = sw[POOL_PAD:] / cnt - u[:, sl]
        outs.append(jnp.dot(dlt.astype(BF16), pw_ref[gi], preferred_element_type=F32))
    ya_ref[...] = (jnp.concatenate(outs, axis=-1) * ps_ref[...]).astype(BF16)

    r_in = r_ref[...]
    k_in = k_ref[...]
    v_in = v_ref[...]
    lo_in = l_ref[...]
    r = _shift_mix(r_in, carry[0:1, :], mu_ref[0:1, :])
    xk = _shift_mix(k_in, carry[1:2, :], mu_ref[1:2, :])
    v = _shift_mix(v_in, carry[2:3, :], mu_ref[2:3, :])
    xl = _shift_mix(lo_in, carry_l[0:1, :], mul_ref[...])
    last = jnp.concatenate([r_in[c - 1:c, :], k_in[c - 1:c, :], v_in[c - 1:c, :]], axis=0)
    carry[0:3, :] = last
    carry_l[0:1, :] = lo_in[c - 1:c, :]
    ns_ref[0] = last
    nsl_ref[0] = lo_in[c - 1:c, :]

    seg = seg_ref[...]
    xda = xl[:, 0:LANES]
    xg = xl[:, LANES:LANES + LORA_GATE_PAD]
    zd = jnp.dot(jnp.tanh(xda).astype(BF16), wd_ref[...], preferred_element_type=F32)
    logw = -jnp.exp(-_softplus(-(w0_ref[...] + zd)) - 0.5)
    za = jnp.dot(xda.astype(BF16), wa_ref[...], preferred_element_type=F32)
    a = jax.nn.sigmoid(a0_ref[...] + za)
    g_ref[...] = jnp.dot(jax.nn.sigmoid(xg).astype(BF16), wg_ref[...], preferred_element_type=F32)
    kkr = xk * kkw_ref[...]
    kk = kkr / jnp.maximum(jnp.sqrt(_segsum(kkr * kkr, seg)), 1e-12)
    k = xk * (1.0 + (a - 1.0) * ka_ref[...])
    bb = kk * a

    hi, lo = _split2(logw)
    tri = tri_ref[...]
    linc = jnp.dot(tri, hi, preferred_element_type=F32) + jnp.dot(tri, lo, preferred_element_type=F32)
    lexc = linc - logw
    ltot = linc[c - 1:c, :]
    e_ninc = jnp.exp(-linc)
    kt = kk * jnp.exp(lexc)
    rt = r * jnp.exp(linc)
    bh = bb * e_ninc
    kh = k * e_ninc
    e_rem = jnp.exp(ltot - linc)
    bbar = bb * e_rem
    kbar = k * e_rem
    gtot = jnp.exp(ltot)

    ri = lax.broadcasted_iota(jnp.int32, (2 * c, 2 * c), 0)
    ci = lax.broadcasted_iota(jnp.int32, (2 * c, 2 * c), 1)
    strict = (ci % c) < (ri % c)
    incl = (ci % c) <= (ri % c)
    blk = (ri // SUB) == (ci // SUB)
    eye = ri == ci
    m0 = lax.broadcasted_iota(jnp.int32, (c, LANES), 1) < RW_HEAD

    pairs = range(PAIRS)
    lanes = [slice(p * LANES, (p + 1) * LANES) for p in pairs]
    each = lambda fn, *cols: [fn(*args) for args in zip(*cols)]
    stacked = lambda x: [_stack2(x[:, sl], m0) for sl in lanes]
    kt2, rt2, bh2, kh2, bbar2, kbar2, v2 = (stacked(x) for x in (kt, rt, bh, kh, bbar, kbar, v))

    aa = each(lambda a_, b_, c_, d_: _mm_nt(jnp.concatenate([a_, b_], axis=0), jnp.concatenate([c_, d_], axis=0)),
              kt2, rt2, bh2, kh2)
    lp = [jnp.where(strict, x[0:2 * c, 0:2 * c], 0.0) for x in aa]
    akp = [jnp.where(strict, x[0:2 * c, 2 * c:4 * c], 0.0) for x in aa]
    arb = [jnp.where(incl, x[2 * c:4 * c, 0:2 * c], 0.0) for x in aa]
    ark = [jnp.where(incl, x[2 * c:4 * c, 2 * c:4 * c], 0.0) for x in aa]

    ld = [jnp.where(blk, x, 0.0) for x in lp]
    lo_ = each(lambda x, y_: x - y_, lp, ld)
    p2 = each(_mm, ld, ld)
    p4 = each(_mm, p2, p2)
    p8 = each(_mm, p4, p4)
    e1 = each(lambda a_, b_, m: a_ - b_ - m, p2, ld, each(_mm, ld, p2))
    e2 = each(lambda a_, b_, m: a_ + b_ + m, e1, p4, each(_mm, e1, p4))
    dm = each(lambda a_, b_, m: a_ + b_ + m, e2, p8, each(_mm, e2, p8))
    n1 = each(lambda a_, m: a_ + m, lo_, each(_mm, dm, lo_))
    n2 = each(_mm, n1, n1)
    f = each(lambda a_, b_, m: a_ - b_ - m, n2, n1, each(_mm, n1, n2))
    tm = each(lambda a_, b_, m: a_ + b_ + m, f, dm, each(_mm, f, dm))

    akv = each(_mm, akp, v2)
    uu = each(lambda a_, b_: jnp.concatenate([a_, b_], axis=1), kt2, akv)
    pq = each(lambda a_, m: a_ + m, uu, each(_mm, tm, uu))
    rbpq = each(_mm, arb, pq)
    ry = each(lambda a_, m: a_ - m[:, 0:LANES], rt2, rbpq)
    y0 = each(lambda m, n_: m - n_[:, LANES:2 * LANES], each(_mm, ark, v2), rbpq)
    btpq = each(_mm_tn, bbar2, pq)
    g2 = [jnp.where(eye, gtot[:, sl], 0.0) - m[:, 0:LANES] for sl, m in zip(lanes, btpq)]
    h2 = each(lambda m, n_: m - n_[:, LANES:2 * LANES], each(_mm_tn, kbar2, v2), btpq)

    st = [s_ref[0, p] for p in pairs]
    y2 = each(lambda a_, b_, o: _mm(a_, b_) + o, ry, st, y0)
    s_new = each(lambda a_, b_, o: _mm_hilo(a_, b_) + o, g2, st, h2)
    for p in pairs:
        s_ref[0, p] = s_new[p]
    y = jnp.concatenate([x[0:c] + x[c:2 * c] for x in y2], axis=-1)

    inv_n = 1.0 / RW_HEAD
    mean = _segsum(y, seg) * inv_n
    yc = y - mean
    var = _segsum(yc * yc, seg) * inv_n
    yn = yc * lax.rsqrt(var + GN_EPS) * gain_ref[...] + bias_ref[...]
    bonus = _segsum(r * k * rk_ref[...], seg)
    yb_ref[...] = yn + bonus * v


def seq_mixers(proj, shift_rkv, shift_lora, pool_past, s0_blk, lw, *, prompt_batch, prompt_len, sample_len):
    c = CHUNK
    n_seq = shift_rkv.shape[0]
    tiles_p = prompt_len // c
    tiles_s = sample_len // c
    prompt_tiles = prompt_batch * tiles_p
    m = proj.shape[0]
    n_tiles = m // c
    seq = lambda i: jnp.where(i < prompt_tiles, i // tiles_p, prompt_batch + (i - prompt_tiles) // tiles_s)
    col = lambda width, cb: pl.BlockSpec((c, width), lambda i: (i, cb))
    per_seq = lambda *shape: pl.BlockSpec((1,) + shape, lambda i: (seq(i),) + tuple(0 for _ in shape))
    full = lambda *shape: pl.BlockSpec(shape, lambda i: tuple(0 for _ in shape))
    tri = (jnp.arange(c)[:, None] >= jnp.arange(c)[None, :]).astype(BF16)
    row = lambda v: v.reshape(1, -1)
    return pl.pallas_call(
        functools.partial(_seq_kernel, prompt_tiles=prompt_tiles, tiles_p=tiles_p, tiles_s=tiles_s),
        out_shape=(jax.ShapeDtypeStruct((m, POOL_WIDTH), BF16),
                   jax.ShapeDtypeStruct((m, RW_WIDTH), F32),
                   jax.ShapeDtypeStruct((m, RW_WIDTH), F32),
                   jax.ShapeDtypeStruct((n_seq, 3, RW_WIDTH), F32),
                   jax.ShapeDtypeStruct((n_seq, 1, LORA_PAD), F32),
                   jax.ShapeDtypeStruct((n_seq, POOL_PAD, POOL_WIDTH), F32),
                   jax.ShapeDtypeStruct((n_seq, PAIRS, LANES, LANES), F32)),
        grid=(n_tiles,),
        in_specs=[
            col(RW_WIDTH, 0), col(RW_WIDTH, 1), col(RW_WIDTH, 2), col(LORA_PAD, COL_LORA // LORA_PAD),
            col(POOL_WIDTH, COL_POOL // POOL_WIDTH),
            per_seq(3, RW_WIDTH), per_seq(1, LORA_PAD), per_seq(POOL_PAD, POOL_WIDTH), per_seq(PAIRS, LANES, LANES),
            full(3, RW_WIDTH), full(1, LORA_PAD),
            full(1, RW_WIDTH), full(1, RW_WIDTH), full(1, RW_WIDTH), full(1, RW_WIDTH),
            full(LANES, RW_WIDTH), full(LANES, RW_WIDTH), full(LORA_GATE_PAD, RW_WIDTH),
            full(POOL_GROUPS, POOL_GC, POOL_GC), full(1, POOL_WIDTH),
            full(1, RW_WIDTH), full(1, RW_WIDTH), full(1, RW_WIDTH),
            full(LANES, LANES), full(c, c),
        ],
        out_specs=(col(POOL_WIDTH, 0), col(RW_WIDTH, 0), col(RW_WIDTH, 0),
                   per_seq(3, RW_WIDTH), per_seq(1, LORA_PAD), per_seq(POOL_PAD, POOL_WIDTH),
                   per_seq(PAIRS, LANES, LANES)),
        scratch_shapes=[pltpu.VMEM((8, RW_WIDTH), F32), pltpu.VMEM((8, LORA_PAD), F32),
                        pltpu.VMEM((POOL_PAD, POOL_WIDTH), F32)],
        compiler_params=_params(("arbitrary",)),
        name="seq_mixers",
    )(proj, proj, proj, proj, proj, shift_rkv, shift_lora, pool_past, s0_blk,
      lw["mu_rkv"], lw["mu_lora"], lw["w0"], lw["a0"], lw["k_k"], lw["k_a"], lw["wd"], lw["wa"], lw["wg"],
      lw["pool_w"], row(lw["pool_scale"]), row(lw["gn_gain"]), row(lw["gn_bias"]), row(lw["r_k"]), lw["seg"], tri)


def _state_to_blocks(s0):
    b = s0.shape[0]
    st = jnp.swapaxes(s0, -1, -2).reshape(b, PAIRS, 2, RW_HEAD, RW_HEAD)
    z = jnp.zeros_like(st[:, :, 0])
    top = jnp.concatenate([st[:, :, 0], z], axis=-1)
    bottom = jnp.concatenate([z, st[:, :, 1]], axis=-1)
    return jnp.concatenate([top, bottom], axis=-2)


def _blocks_to_state(sb):
    b = sb.shape[0]
    d = jnp.stack([sb[:, :, :RW_HEAD, :RW_HEAD], sb[:, :, RW_HEAD:, RW_HEAD:]], axis=2)
    return jnp.swapaxes(d.reshape(b, RW_HEADS, RW_HEAD, RW_HEAD), -1, -2)


def _stacked_weights(w_in, pool_w, w_decay_up, w_aaa_up, w_gate_up, proj_pool, proj_rwkv, w_out,
                     w_ffn_gate, w_ffn_up, w_ffn_down, w_ple_gate, w_ple_proj):
    c_rw = POOL_WIDTH
    c_lora = POOL_WIDTH + 3 * RW_WIDTH
    c_gp = POOL_WIDTH + RW_PROJ
    w_t = jnp.swapaxes(w_in, 1, 2)
    w_in_t = jnp.concatenate([
        w_t[:, c_rw:c_lora], w_t[:, :c_rw], w_t[:, c_lora:c_gp],
        jnp.zeros((DEPTH, LORA_PAD - LORA, D_MODEL), F32), w_t[:, c_gp:]], axis=1).astype(BF16)
    zpad = lambda rows: jnp.zeros((DEPTH, rows, RW_WIDTH), F32)
    return dict(
        w_in=w_in_t, pool_w=pool_w.astype(BF16),
        wd=jnp.concatenate([w_decay_up, zpad(LANES - DECAY_LORA)], axis=1).astype(BF16),
        wa=jnp.concatenate([zpad(DECAY_LORA), w_aaa_up], axis=1).astype(BF16),
        wg=jnp.concatenate([w_gate_up, zpad(LORA_GATE_PAD - GATE_LORA)], axis=1).astype(BF16),
        proj_pool=proj_pool.astype(BF16), proj_rwkv=proj_rwkv.astype(BF16), w_out=w_out.astype(BF16),
        w_ffn_gate=w_ffn_gate.astype(BF16), w_ffn_up=w_ffn_up.astype(BF16), w_ffn_down=w_ffn_down.astype(BF16),
        w_ple_gate=w_ple_gate.astype(BF16), w_ple_proj=w_ple_proj.astype(BF16))


def kernel(x_prompt, x_sample, state_shift, state_pool, state_wkv, p_prompt, p_sample, norm_mix, w_in, mu_shift, pool_w, pool_scale, w0, w_decay_up, a0, w_aaa_up, w_gate_up, k_k, k_a, r_k, gn_gain, gn_bias, proj_pool, proj_rwkv, w_out, norm_ffn, w_ffn_gate, w_ffn_up, w_ffn_down, norm_ple, w_ple_gate, w_ple_proj, norm_final):
    bp, tp, _ = x_prompt.shape
    bs, ts, _ = x_sample.shape
    mp = bp * tp
    ms = bs * ts
    x = jnp.concatenate([x_prompt.reshape(mp, D_MODEL), x_sample.reshape(ms, D_MODEL)], axis=0)
    pp_tok = p_prompt.reshape(DEPTH, mp, PLE_DIM)
    ps_tok = p_sample.reshape(DEPTH, ms, PLE_DIM)
    sw = _stacked_weights(w_in, pool_w, w_decay_up, w_aaa_up, w_gate_up, proj_pool, proj_rwkv, w_out,
                          w_ffn_gate, w_ffn_up, w_ffn_down, w_ple_gate, w_ple_proj)
    lane = jnp.arange(LANES) // RW_HEAD
    seg = (lane[:, None] == lane[None, :]).astype(BF16)
    lead = lambda a: jnp.concatenate([jnp.zeros((DEPTH, bp) + a.shape[2:], a.dtype), a], axis=1)
    shift_all = lead(state_shift)
    shift_rkv = shift_all[:, :, :3 * RW_WIDTH].reshape(DEPTH, bp + bs, 3, RW_WIDTH)
    shift_lora = jnp.pad(shift_all[:, :, 3 * RW_WIDTH:], ((0, 0), (0, 0), (0, LORA_PAD - LORA)))
    shift_lora = shift_lora.reshape(DEPTH, bp + bs, 1, LORA_PAD)
    pool_all = jnp.pad(lead(state_pool), ((0, 0), (0, 0), (POOL_PAD - POOL_PAST, 0), (0, 0)))
    wkv_all = lead(state_wkv)
    row = lambda v: v.reshape(1, -1)
    new_shift, new_pool, new_wkv = [], [], []
    for i in range(DEPTH):
        mu = mu_shift[i]
        lw = dict(
            mu_rkv=mu[:3 * RW_WIDTH].reshape(3, RW_WIDTH),
            mu_lora=jnp.pad(mu[3 * RW_WIDTH:], (0, LORA_PAD - LORA)).reshape(1, LORA_PAD),
            pool_w=sw["pool_w"][i], pool_scale=pool_scale[i],
            w0=row(w0[i]), a0=row(a0[i]), k_k=row(k_k[i]), k_a=row(k_a[i]),
            wd=sw["wd"][i], wa=sw["wa"][i], wg=sw["wg"][i], seg=seg,
            r_k=r_k[i].reshape(RW_WIDTH), gn_gain=gn_gain[i], gn_bias=gn_bias[i])
        proj, gates = in_proj(x, norm_mix[i], sw["w_in"], i)
        ya, yb, g, ns_rkv, ns_lora, pool_rows, s_blk = seq_mixers(
            proj, shift_rkv[i], shift_lora[i], pool_all[i], _state_to_blocks(wkv_all[i]), lw,
            prompt_batch=bp, prompt_len=tp, sample_len=ts)
        new_shift.append(jnp.concatenate([ns_rkv.reshape(bp + bs, 3 * RW_WIDTH), ns_lora[:, 0, :LORA]], axis=1))
        new_pool.append(pool_rows[:, POOL_PAD - POOL_PAST:])
        new_wkv.append(_blocks_to_state(s_blk))
        x = merge_out(ya, yb, g, gates, x, sw["proj_pool"], sw["proj_rwkv"], sw["w_out"], i)
        x = ffn(x, norm_ffn[i], sw["w_ffn_gate"], sw["w_ffn_up"], sw["w_ffn_down"], i)
        x = ple_update(x, norm_ple[i], sw["w_ple_gate"], pp_tok, ps_tok, sw["w_ple_proj"], i)
    y_prompt = final_norm(x, norm_final, row0=0, rows=mp).reshape(bp, tp, D_MODEL)
    y_sample = final_norm(x, norm_final, row0=mp, rows=ms).reshape(bs, ts, D_MODEL)
    new_shift, new_pool, new_wkv = jnp.stack(new_shift), jnp.stack(new_pool), jnp.stack(new_wkv)
    return (y_prompt, y_sample, new_shift[:, :bp], new_pool[:, :bp], new_wkv[:, :bp],
            new_shift[:, bp:], new_pool[:, bp:], new_wkv[:, bp:])
```

```python
import functools

import jax
import jax.numpy as jnp
from jax import lax
from jax.experimental import pallas as pl
from jax.experimental.pallas import tpu as pltpu

F32 = jnp.float32
BF16 = jnp.bfloat16

D_MODEL = 2048
DEPTH = 4
PAST_LEN = 1024
PLE_DIM = 256
POOL_WINDOWS = (2, 4, 8, 16)
POOL_GROUPS = 4
POOL_WIDTH = D_MODEL // 2
POOL_GC = POOL_WIDTH // POOL_GROUPS
POOL_PAST = max(POOL_WINDOWS) - 1
RW_WIDTH = D_MODEL // 2
RW_HEAD = 64
RW_HEADS = RW_WIDTH // RW_HEAD
DECAY_LORA = 64
AAA_LORA = 64
GATE_LORA = 160
LORA = DECAY_LORA + AAA_LORA + GATE_LORA
RW_PROJ = 3 * RW_WIDTH + LORA
D_FF = 5632
NORM_EPS = 1e-6
GN_EPS = 64e-5

LANES = 128
POOL_PAD = 16
LORA_PAD = 512
LORA_GATE_PAD = 256
PAIRS = RW_WIDTH // LANES
CHUNK = 64
SUB = 16
COL_RKV = 0
COL_POOL = 3 * RW_WIDTH
COL_LORA = COL_POOL + POOL_WIDTH
MAIN_COLS = COL_LORA + LORA_PAD
GATE_COLS = 2 * D_MODEL

TN = 512
TM_WIDE = 1536
TM_ROW = 768
TM_MERGE = 384
TM_PLE = 512
VMEM_LIMIT = 56 * 1024 * 1024


def _params(sem):
    return pltpu.CompilerParams(dimension_semantics=sem, vmem_limit_bytes=VMEM_LIMIT)


def _rms(x, g):
    ms = jnp.mean(x * x, axis=-1, keepdims=True)
    return x * lax.rsqrt(ms + NORM_EPS) * g


def _wspec(k, n, layer, col_of):
    return pl.BlockSpec((None, k, n), lambda *idx: (layer, 0, col_of(*idx)))


def _in_proj_kernel(x_ref, g_ref, wt_ref, main_ref, gates_ref, h_scr):
    @pl.when(pl.program_id(1) == 0)
    def _():
        h_scr[...] = _rms(x_ref[...], g_ref[...]).astype(BF16)

    acc = lax.dot_general(h_scr[...], wt_ref[...], (((1,), (1,)), ((), ())), preferred_element_type=F32)
    main_ref[...] = acc
    gates_ref[...] = acc.astype(BF16)


def in_proj(x, g, wt, layer):
    m, k = x.shape
    tm = TM_WIDE
    main_tiles = MAIN_COLS // TN
    gate_tiles = GATE_COLS // TN
    return pl.pallas_call(
        _in_proj_kernel,
        out_shape=(jax.ShapeDtypeStruct((m, MAIN_COLS + TN), F32), jax.ShapeDtypeStruct((m, GATE_COLS + TN), BF16)),
        grid=(m // tm, main_tiles + gate_tiles),
        in_specs=[
            pl.BlockSpec((tm, k), lambda i, j: (i, 0)),
            pl.BlockSpec((1, k), lambda i, j: (0, 0)),
            pl.BlockSpec((None, TN, k), lambda i, j: (layer, j, 0)),
        ],
        out_specs=(pl.BlockSpec((tm, TN), lambda i, j: (i, jnp.minimum(j, main_tiles))),
                   pl.BlockSpec((tm, TN), lambda i, j: (i, jnp.where(j < main_tiles, gate_tiles, j - main_tiles)))),
        scratch_shapes=[pltpu.VMEM((tm, k), BF16)],
        compiler_params=_params(("parallel", "arbitrary")),
        name="in_proj",
    )(x, g.reshape(1, k), wt)


def _merge_out_kernel(ya_ref, yb_ref, g_ref, gates_ref, x_ref, wp_ref, wr_ref, wo_ref, o_ref, ybg_scr, m_scr):
    n = o_ref.shape[1]
    ybg_scr[...] = (yb_ref[...] * g_ref[...]).astype(BF16)
    for c in range(n // TN):
        sl = slice(c * TN, (c + 1) * TN)
        sr = slice(n + c * TN, n + (c + 1) * TN)
        a = jnp.dot(ya_ref[...], wp_ref[:, sl], preferred_element_type=F32)
        b = jnp.dot(ybg_scr[...], wr_ref[:, sl], preferred_element_type=F32)
        gate_a = jax.nn.sigmoid(gates_ref[:, sl].astype(F32))
        gate_b = jax.nn.sigmoid(gates_ref[:, sr].astype(F32))
        m_scr[:, sl] = (gate_a * a + gate_b * b).astype(BF16)
    for c in range(n // TN):
        sl = slice(c * TN, (c + 1) * TN)
        o_ref[:, sl] = x_ref[:, sl] + jnp.dot(m_scr[...], wo_ref[:, sl], preferred_element_type=F32)


def merge_out(ya, yb, g, gates, x, wp, wr, wo, layer):
    m, n = x.shape
    tm = TM_MERGE
    resident = lambda k_: pl.BlockSpec((None, k_, n), lambda i: (layer, 0, 0), pipeline_mode=pl.Buffered(1))
    return pl.pallas_call(
        _merge_out_kernel,
        out_shape=jax.ShapeDtypeStruct((m, n), F32),
        grid=(m // tm,),
        in_specs=[
            pl.BlockSpec((tm, POOL_WIDTH), lambda i: (i, 0)),
            pl.BlockSpec((tm, RW_WIDTH), lambda i: (i, 0)),
            pl.BlockSpec((tm, RW_WIDTH), lambda i: (i, 0)),
            pl.BlockSpec((tm, GATE_COLS), lambda i: (i, 0)),
            pl.BlockSpec((tm, n), lambda i: (i, 0)),
            resident(POOL_WIDTH), resident(RW_WIDTH), resident(n),
        ],
        out_specs=pl.BlockSpec((tm, n), lambda i: (i, 0)),
        scratch_shapes=[pltpu.VMEM((tm, RW_WIDTH), BF16), pltpu.VMEM((tm, n), BF16)],
        compiler_params=_params(("parallel",)),
        name="merge_out",
    )(ya, yb, g, gates, x, wp, wr, wo)


def _ffn_kernel(x_ref, g_ref, wg_ref, wu_ref, wd_ref, o_ref, h_scr):
    @pl.when(pl.program_id(1) == 0)
    def _():
        x = x_ref[...]
        h_scr[...] = _rms(x, g_ref[...]).astype(BF16)
        o_ref[...] = x

    h = h_scr[...]
    gate = jnp.dot(h, wg_ref[...], preferred_element_type=F32)
    up = jnp.dot(h, wu_ref[...], preferred_element_type=F32)
    act = (gate * jax.nn.sigmoid(gate) * up).astype(BF16)
    o_ref[...] += jnp.dot(act, wd_ref[...], preferred_element_type=F32)


def ffn(x, g, wg, wu, wd, layer):
    m, k = x.shape
    f = wg.shape[2]
    tm = TM_ROW
    return pl.pallas_call(
        _ffn_kernel,
        out_shape=jax.ShapeDtypeStruct((m, k), F32),
        grid=(m // tm, f // TN),
        in_specs=[
            pl.BlockSpec((tm, k), lambda i, j: (i, 0)),
            pl.BlockSpec((1, k), lambda i, j: (0, 0)),
            _wspec(k, TN, layer, lambda i, j: j),
            _wspec(k, TN, layer, lambda i, j: j),
            pl.BlockSpec((None, TN, k), lambda i, j: (layer, j, 0)),
        ],
        out_specs=pl.BlockSpec((tm, k), lambda i, j: (i, 0)),
        scratch_shapes=[pltpu.VMEM((tm, k), BF16)],
        compiler_params=_params(("parallel", "arbitrary")),
        name="ffn",
    )(x, g.reshape(1, k), wg, wu, wd)


def _ple_kernel(x_ref, g_ref, wg_ref, pp_ref, ps_ref, wp_ref, o_ref, h_scr, *, prompt_tiles):
    n = o_ref.shape[1]
    h_scr[...] = _rms(x_ref[...], g_ref[...]).astype(BF16)
    pb = jnp.where(pl.program_id(0) < prompt_tiles, pp_ref[...], ps_ref[...]).astype(BF16)
    for c in range(n // TN):
        sl = slice(c * TN, (c + 1) * TN)
        gate = jnp.dot(h_scr[...], wg_ref[:, sl], preferred_element_type=F32)
        emb = jnp.dot(pb, wp_ref[:, sl], preferred_element_type=F32)
        o_ref[:, sl] = x_ref[:, sl] + jax.nn.sigmoid(gate) * emb


def ple_update(x, g, wg, p_prompt, p_sample, wp, layer):
    m, k = x.shape
    n = wg.shape[2]
    kp = p_prompt.shape[2]
    tm = TM_PLE
    prompt_tiles = p_prompt.shape[1] // tm
    sample_tiles = p_sample.shape[1] // tm
    return pl.pallas_call(
        functools.partial(_ple_kernel, prompt_tiles=prompt_tiles),
        out_shape=jax.ShapeDtypeStruct((m, n), F32),
        grid=(m // tm,),
        in_specs=[
            pl.BlockSpec((tm, k), lambda i: (i, 0)),
            pl.BlockSpec((1, k), lambda i: (0, 0)),
            _wspec(k, n, layer, lambda i: 0),
            pl.BlockSpec((None, tm, kp), lambda i: (layer, jnp.minimum(i, prompt_tiles - 1), 0)),
            pl.BlockSpec((None, tm, kp), lambda i: (layer, jnp.clip(i - prompt_tiles, 0, sample_tiles - 1), 0)),
            _wspec(kp, n, layer, lambda i: 0),
        ],
        out_specs=pl.BlockSpec((tm, n), lambda i: (i, 0)),
        scratch_shapes=[pltpu.VMEM((tm, k), BF16)],
        compiler_params=_params(("parallel",)),
        name="ple_update",
    )(x, g.reshape(1, k), wg, p_prompt, p_sample, wp)


def _final_norm_kernel(x_ref, g_ref, o_ref):
    o_ref[...] = _rms(x_ref[...], g_ref[...])


def final_norm(x, g, *, row0, rows):
    k = x.shape[1]
    tm = 512
    rb0 = row0 // tm
    return pl.pallas_call(
        _final_norm_kernel,
        out_shape=jax.ShapeDtypeStruct((rows, k), F32),
        grid=(rows // tm,),
        in_specs=[pl.BlockSpec((tm, k), lambda i: (rb0 + i, 0)), pl.BlockSpec((1, k), lambda i: (0, 0))],
        out_specs=pl.BlockSpec((tm, k), lambda i: (i, 0)),
        compiler_params=_params(("parallel",)),
        name="final_norm",
    )(x, g.reshape(1, k))


def _shift_mix(cur, carry_row, mu):
    rolled = pltpu.roll(cur, 1, axis=0)
    row = lax.broadcasted_iota(jnp.int32, cur.shape, 0)
    prev = jnp.where(row == 0, carry_row, rolled)
    return cur + (prev - cur) * mu


def _softplus(x):
    return jnp.maximum(x, 0.0) + jnp.log1p(jnp.exp(-jnp.abs(x)))


def _mm(a, b):
    return jnp.dot(a.astype(BF16), b.astype(BF16), preferred_element_type=F32)


def _mm_nt(a, b):
    return lax.dot_general(a.astype(BF16), b.astype(BF16), (((1,), (1,)), ((), ())),
                           preferred_element_type=F32)


def _mm_tn(a, b):
    return lax.dot_general(a.astype(BF16), b.astype(BF16), (((0,), (0,)), ((), ())),
                           preferred_element_type=F32)


def _split2(x):
    hi = x.astype(BF16)
    lo = (x - hi.astype(F32)).astype(BF16)
    return hi, lo


def _mm_hilo(a, b):
    ah, al = _split2(a)
    bh, bl = _split2(b)
    d = lambda x, y: jnp.dot(x, y, preferred_element_type=F32)
    return d(ah, bh) + d(ah, bl) + d(al, bh)


def _segsum(x, seg):
    xb = x.astype(BF16)
    parts = [jnp.dot(xb[:, c * LANES:(c + 1) * LANES], seg, preferred_element_type=F32) for c in range(PAIRS)]
    return jnp.concatenate(parts, axis=-1)


def _stack2(x, m0):
    return jnp.concatenate([jnp.where(m0, x, 0.0), jnp.where(m0, 0.0, x)], axis=0)


def _seq_kernel(r_ref, k_ref, v_ref, l_ref, u_ref, sp_ref, spl_ref, past_ref, s0_ref,
                mu_ref, mul_ref, w0_ref, a0_ref, kkw_ref, ka_ref, wd_ref, wa_ref, wg_ref,
                pw_ref, ps_ref, gain_ref, bias_ref, rk_ref, seg_ref, tri_ref,
                ya_ref, yb_ref, g_ref, ns_ref, nsl_ref, np_ref, s_ref,
                carry, carry_l, carry_p, *, prompt_tiles, tiles_p, tiles_s):
    c = CHUNK
    i = pl.program_id(0)
    in_prompt = i < prompt_tiles
    t = jnp.where(in_prompt, i % tiles_p, (i - prompt_tiles) % tiles_s)
    pos0 = jnp.where(in_prompt, 0, PAST_LEN) + t * c

    @pl.when(t == 0)
    def _():
        carry[0:3, :] = sp_ref[0]
        carry_l[0:1, :] = spl_ref[0]
        carry_p[...] = past_ref[0]
        s_ref[...] = s0_ref[...]

    u = u_ref[...]
    ext = jnp.concatenate([carry_p[...], u], axis=0)
    carry_p[...] = ext[c:c + POOL_PAD]
    np_ref[0] = ext[c:c + POOL_PAD]
    pos = pos0 + lax.broadcasted_iota(jnp.int32, (c, 1), 0)
    outs = []
    for gi, win in enumerate(POOL_WINDOWS):
        sl = slice(gi * POOL_GC, (gi + 1) * POOL_GC)
        sw = ext[:, sl]
        sh = 1
        while sh < win:
            sw = sw + pltpu.roll(sw, sh, axis=0)
            sh *= 2
        cnt = jnp.minimum(pos + 1, win).astype(F32)
        dlt = sw[POOL_PAD:] / cnt - u[:, sl]
        outs.append(jnp.dot(dlt.astype(BF16), pw_ref[gi], preferred_element_type=F32))
    ya_ref[...] = (jnp.concatenate(outs, axis=-1) * ps_ref[...]).astype(BF16)

    r_in = r_ref[...]
    k_in = k_ref[...]
    v_in = v_ref[...]
    lo_in = l_ref[...]
    r = _shift_mix(r_in, carry[0:1, :], mu_ref[0:1, :])
    xk = _shift_mix(k_in, carry[1:2, :], mu_ref[1:2, :])
    v = _shift_mix(v_in, carry[2:3, :], mu_ref[2:3, :])
    xl = _shift_mix(lo_in, carry_l[0:1, :], mul_ref[...])
    last = jnp.concatenate([r_in[c - 1:c, :], k_in[c - 1:c, :], v_in[c - 1:c, :]], axis=0)
    carry[0:3, :] = last
    carry_l[0:1, :] = lo_in[c - 1:c, :]
    ns_ref[0] = last
    nsl_ref[0] = lo_in[c - 1:c, :]

    seg = seg_ref[...]
    xda = xl[:, 0:LANES]
    xg = xl[:, LANES:LANES + LORA_GATE_PAD]
    zd = jnp.dot(jnp.tanh(xda).astype(BF16), wd_ref[...], preferred_element_type=F32)
    logw = -jnp.exp(-_softplus(-(w0_ref[...] + zd)) - 0.5)
    za = jnp.dot(xda.astype(BF16), wa_ref[...], preferred_element_type=F32)
    a = jax.nn.sigmoid(a0_ref[...] + za)
    g_ref[...] = jnp.dot(jax.nn.sigmoid(xg).astype(BF16), wg_ref[...], preferred_element_type=F32)
    kkr = xk * kkw_ref[...]
    kk = kkr / jnp.maximum(jnp.sqrt(_segsum(kkr * kkr, seg)), 1e-12)
    k = xk * (1.0 + (a - 1.0) * ka_ref[...])
    bb = kk * a

    hi, lo = _split2(logw)
    tri = tri_ref[...]
    linc = jnp.dot(tri, hi, preferred_element_type=F32) + jnp.dot(tri, lo, preferred_element_type=F32)
    lexc = linc - logw
    ltot = linc[c - 1:c, :]
    e_ninc = jnp.exp(-linc)
    kt = kk * jnp.exp(lexc)
    rt = r * jnp.exp(linc)
    bh = bb * e_ninc
    kh = k * e_ninc
    e_rem = jnp.exp(ltot - linc)
    bbar = bb * e_rem
    kbar = k * e_rem
    gtot = jnp.exp(ltot)

    ri = lax.broadcasted_iota(jnp.int32, (2 * c, 2 * c), 0)
    ci = lax.broadcasted_iota(jnp.int32, (2 * c, 2 * c), 1)
    strict = (ci % c) < (ri % c)
    incl = (ci % c) <= (ri % c)
    blk = (ri // SUB) == (ci // SUB)
    eye = ri == ci
    m0 = lax.broadcasted_iota(jnp.int32, (c, LANES), 1) < RW_HEAD

    pairs = range(PAIRS)
    lanes = [slice(p * LANES, (p + 1) * LANES) for p in pairs]
    each = lambda fn, *cols: [fn(*args) for args in zip(*cols)]
    stacked = lambda x: [_stack2(x[:, sl], m0) for sl in lanes]
    kt2, rt2, bh2, kh2, bbar2, kbar2, v2 = (stacked(x) for x in (kt, rt, bh, kh, bbar, kbar, v))

    aa = each(lambda a_, b_, c_, d_: _mm_nt(jnp.concatenate([a_, b_], axis=0), jnp.concatenate([c_, d_], axis=0)),
              kt2, rt2, bh2, kh2)
    lp = [jnp.where(strict, x[0:2 * c, 0:2 * c], 0.0) for x in aa]
    akp = [jnp.where(strict, x[0:2 * c, 2 * c:4 * c], 0.0) for x in aa]
    arb = [jnp.where(incl, x[2 * c:4 * c, 0:2 * c], 0.0) for x in aa]
    ark = [jnp.where(incl, x[2 * c:4 * c, 2 * c:4 * c], 0.0) for x in aa]

    ld = [jnp.where(blk, x, 0.0) for x in lp]
    lo_ = each(lambda x, y_: x - y_, lp, ld)
    p2 = each(_mm, ld, ld)
    p4 = each(_mm, p2, p2)
    p8 = each(_mm, p4, p4)
    e1 = each(lambda a_, b_, m: a_ - b_ - m, p2, ld, each(_mm, ld, p2))
    e2 = each(lambda a_, b_, m: a_ + b_ + m, e1, p4, each(_mm, e1, p4))
    dm = each(lambda a_, b_, m: a_ + b_ + m, e2, p8, each(_mm, e2, p8))
    n1 = each(lambda a_, m: a_ + m, lo_, each(_mm, dm, lo_))
    n2 = each(_mm, n1, n1)
    f = each(lambda a_, b_, m: a_ - b_ - m, n2, n1, each(_mm, n1, n2))
    tm = each(lambda a_, b_, m: a_ + b_ + m, f, dm, each(_mm, f, dm))

    akv = each(_mm, akp, v2)
    uu = each(lambda a_, b_: jnp.concatenate([a_, b_], axis=1), kt2, akv)
    pq = each(lambda a_, m: a_ + m, uu, each(_mm, tm, uu))
    rbpq = each(_mm, arb, pq)
    ry = each(lambda a_, m: a_ - m[:, 0:LANES], rt2, rbpq)
    y0 = each(lambda m, n_: m - n_[:, LANES:2 * LANES], each(_mm, ark, v2), rbpq)
    btpq = each(_mm_tn, bbar2, pq)
    g2 = [jnp.where(eye, gtot[:, sl], 0.0) - m[:, 0:LANES] for sl, m in zip(lanes, btpq)]
    h2 = each(lambda m, n_: m - n_[:, LANES:2 * LANES], each(_mm_tn, kbar2, v2), btpq)

    st = [s_ref[0, p] for p in pairs]
    y2 = each(lambda a_, b_, o: _mm(a_, b_) + o, ry, st, y0)
    s_new = each(lambda a_, b_, o: _mm(a_, b_) + o, g2, st, h2)
    for p in pairs:
        s_ref[0, p] = s_new[p]
    y = jnp.concatenate([x[0:c] + x[c:2 * c] for x in y2], axis=-1)

    inv_n = 1.0 / RW_HEAD
    mean = _segsum(y, seg) * inv_n
    yc = y - mean
    var = _segsum(yc * yc, seg) * inv_n
    yn = yc * lax.rsqrt(var + GN_EPS) * gain_ref[...] + bias_ref[...]
    bonus = _segsum(r * k * rk_ref[...], seg)
    yb_ref[...] = yn + bonus * v


def seq_mixers(proj, shift_rkv, shift_lora, pool_past, s0_blk, lw, *, prompt_batch, prompt_len, sample_len):
    c = CHUNK
    n_seq = shift_rkv.shape[0]
    tiles_p = prompt_len // c
    tiles_s = sample_len // c
    prompt_tiles = prompt_batch * tiles_p
    m = proj.shape[0]
    n_tiles = m // c
    seq = lambda i: jnp.where(i < prompt_tiles, i // tiles_p, prompt_batch + (i - prompt_tiles) // tiles_s)
    col = lambda width, cb: pl.BlockSpec((c, width), lambda i: (i, cb))
    per_seq = lambda *shape: pl.BlockSpec((1,) + shape, lambda i: (seq(i),) + tuple(0 for _ in shape))
    full = lambda *shape: pl.BlockSpec(shape, lambda i: tuple(0 for _ in shape))
    tri = (jnp.arange(c)[:, None] >= jnp.arange(c)[None, :]).astype(BF16)
    row = lambda v: v.reshape(1, -1)
    return pl.pallas_call(
        functools.partial(_seq_kernel, prompt_tiles=prompt_tiles, tiles_p=tiles_p, tiles_s=tiles_s),
        out_shape=(jax.ShapeDtypeStruct((m, POOL_WIDTH), BF16),
                   jax.ShapeDtypeStruct((m, RW_WIDTH), F32),
                   jax.ShapeDtypeStruct((m, RW_WIDTH), F32),
                   jax.ShapeDtypeStruct((n_seq, 3, RW_WIDTH), F32),
                   jax.ShapeDtypeStruct((n_seq, 1, LORA_PAD), F32),
                   jax.ShapeDtypeStruct((n_seq, POOL_PAD, POOL_WIDTH), F32),
                   jax.ShapeDtypeStruct((n_seq, PAIRS, LANES, LANES), F32)),
        grid=(n_tiles,),
        in_specs=[
            col(RW_WIDTH, 0), col(RW_WIDTH, 1), col(RW_WIDTH, 2), col(LORA_PAD, COL_LORA // LORA_PAD),
            col(POOL_WIDTH, COL_POOL // POOL_WIDTH),
            per_seq(3, RW_WIDTH), per_seq(1, LORA_PAD), per_seq(POOL_PAD, POOL_WIDTH), per_seq(PAIRS, LANES, LANES),
            full(3, RW_WIDTH), full(1, LORA_PAD),
            full(1, RW_WIDTH), full(1, RW_WIDTH), full(1, RW_WIDTH), full(1, RW_WIDTH),
            full(LANES, RW_WIDTH), full(LANES, RW_WIDTH), full(LORA_GATE_PAD, RW_WIDTH),
            full(POOL_GROUPS, POOL_GC, POOL_GC), full(1, POOL_WIDTH),
            full(1, RW_WIDTH), full(1, RW_WIDTH), full(1, RW_WIDTH),
            full(LANES, LANES), full(c, c),
        ],
        out_specs=(col(POOL_WIDTH, 0), col(RW_WIDTH, 0), col(RW_WIDTH, 0),
                   per_seq(3, RW_WIDTH), per_seq(1, LORA_PAD), per_seq(POOL_PAD, POOL_WIDTH),
                   per_seq(PAIRS, LANES, LANES)),
        scratch_shapes=[pltpu.VMEM((8, RW_WIDTH), F32), pltpu.VMEM((8, LORA_PAD), F32),
                        pltpu.VMEM((POOL_PAD, POOL_WIDTH), F32)],
        compiler_params=_params(("arbitrary",)),
        name="seq_mixers",
    )(proj, proj, proj, proj, proj, shift_rkv, shift_lora, pool_past, s0_blk,
      lw["mu_rkv"], lw["mu_lora"], lw["w0"], lw["a0"], lw["k_k"], lw["k_a"], lw["wd"], lw["wa"], lw["wg"],
      lw["pool_w"], row(lw["pool_scale"]), row(lw["gn_gain"]), row(lw["gn_bias"]), row(lw["r_k"]), lw["seg"], tri)


def _state_to_blocks(s0):
    b = s0.shape[0]
    st = jnp.swapaxes(s0, -1, -2).reshape(b, PAIRS, 2, RW_HEAD, RW_HEAD)
    z = jnp.zeros_like(st[:, :, 0])
    top = jnp.concatenate([st[:, :, 0], z], axis=-1)
    bottom = jnp.concatenate([z, st[:, :, 1]], axis=-1)
    return jnp.concatenate([top, bottom], axis=-2)


def _blocks_to_state(sb):
    b = sb.shape[0]
    d = jnp.stack([sb[:, :, :RW_HEAD, :RW_HEAD], sb[:, :, RW_HEAD:, RW_HEAD:]], axis=2)
    return jnp.swapaxes(d.reshape(b, RW_HEADS, RW_HEAD, RW_HEAD), -1, -2)


def _stacked_weights(w_in, pool_w, w_decay_up, w_aaa_up, w_gate_up, proj_pool, proj_rwkv, w_out,
                     w_ffn_gate, w_ffn_up, w_ffn_down, w_ple_gate, w_ple_proj):
    c_rw = POOL_WIDTH
    c_lora = POOL_WIDTH + 3 * RW_WIDTH
    c_gp = POOL_WIDTH + RW_PROJ
    w_t = jnp.swapaxes(w_in, 1, 2)
    w_in_t = jnp.concatenate([
        w_t[:, c_rw:c_lora], w_t[:, :c_rw], w_t[:, c_lora:c_gp],
        jnp.zeros((DEPTH, LORA_PAD - LORA, D_MODEL), F32), w_t[:, c_gp:]], axis=1).astype(BF16)
    zpad = lambda rows: jnp.zeros((DEPTH, rows, RW_WIDTH), F32)
    return dict(
        w_in=w_in_t, pool_w=pool_w.astype(BF16),
        wd=jnp.concatenate([w_decay_up, zpad(LANES - DECAY_LORA)], axis=1).astype(BF16),
        wa=jnp.concatenate([zpad(DECAY_LORA), w_aaa_up], axis=1).astype(BF16),
        wg=jnp.concatenate([w_gate_up, zpad(LORA_GATE_PAD - GATE_LORA)], axis=1).astype(BF16),
        proj_pool=proj_pool.astype(BF16), proj_rwkv=proj_rwkv.astype(BF16), w_out=w_out.astype(BF16),
        w_ffn_gate=w_ffn_gate.astype(BF16), w_ffn_up=w_ffn_up.astype(BF16), w_ffn_down=w_ffn_down.astype(BF16),
        w_ple_gate=w_ple_gate.astype(BF16), w_ple_proj=w_ple_proj.astype(BF16))


def kernel(x_prompt, x_sample, state_shift, state_pool, state_wkv, p_prompt, p_sample, norm_mix, w_in, mu_shift, pool_w, pool_scale, w0, w_decay_up, a0, w_aaa_up, w_gate_up, k_k, k_a, r_k, gn_gain, gn_bias, proj_pool, proj_rwkv, w_out, norm_ffn, w_ffn_gate, w_ffn_up, w_ffn_down, norm_ple, w_ple_gate, w_ple_proj, norm_final):
    bp, tp, _ = x_prompt.shape
    bs, ts, _ = x_sample.shape
    mp = bp * tp
    ms = bs * ts
    x = jnp.concatenate([x_prompt.reshape(mp, D_MODEL), x_sample.reshape(ms, D_MODEL)], axis=0)
    pp_tok = p_prompt.reshape(DEPTH, mp, PLE_DIM)
    ps_tok = p_sample.reshape(DEPTH, ms, PLE_DIM)
    sw = _stacked_weights(w_in, pool_w, w_decay_up, w_aaa_up, w_gate_up, proj_pool, proj_rwkv, w_out,
                          w_ffn_gate, w_ffn_up, w_ffn_down, w_ple_gate, w_ple_proj)
    lane = jnp.arange(LANES) // RW_HEAD
    seg = (lane[:, None] == lane[None, :]).astype(BF16)
    lead = lambda a: jnp.concatenate([jnp.zeros((DEPTH, bp) + a.shape[2:], a.dtype), a], axis=1)
    shift_all = lead(state_shift)
    shift_rkv = shift_all[:, :, :3 * RW_WIDTH].reshape(DEPTH, bp + bs, 3, RW_WIDTH)
    shift_lora = jnp.pad(shift_all[:, :, 3 * RW_WIDTH:], ((0, 0), (0, 0), (0, LORA_PAD - LORA)))
    shift_lora = shift_lora.reshape(DEPTH, bp + bs, 1, LORA_PAD)
    pool_all = jnp.pad(lead(state_pool), ((0, 0), (0, 0), (POOL_PAD - POOL_PAST, 0), (0, 0)))
    wkv_all = lead(state_wkv)
    row = lambda v: v.reshape(1, -1)
    new_shift, new_pool, new_wkv = [], [], []
    for i in range(DEPTH):
        mu = mu_shift[i]
        lw = dict(
            mu_rkv=mu[:3 * RW_WIDTH].reshape(3, RW_WIDTH),
            mu_lora=jnp.pad(mu[3 * RW_WIDTH:], (0, LORA_PAD - LORA)).reshape(1, LORA_PAD),
            pool_w=sw["pool_w"][i], pool_scale=pool_scale[i],
            w0=row(w0[i]), a0=row(a0[i]), k_k=row(k_k[i]), k_a=row(k_a[i]),
            wd=sw["wd"][i], wa=sw["wa"][i], wg=sw["wg"][i], seg=seg,
            r_k=r_k[i].reshape(RW_WIDTH), gn_gain=gn_gain[i], gn_bias=gn_bias[i])
        proj, gates = in_proj(x, norm_mix[i], sw["w_in"], i)
        ya, yb, g, ns_rkv, ns_lora, pool_rows, s_blk = seq_mixers(
            proj, shift_rkv[i], shift_lora[i], pool_all[i], _state_to_blocks(wkv_all[i]), lw,
            prompt_batch=bp, prompt_len=tp, sample_len=ts)
        new_shift.append(jnp.concatenate([ns_rkv.reshape(bp + bs, 3 * RW_WIDTH), ns_lora[:, 0, :LORA]], axis=1))
        new_pool.append(pool_rows[:, POOL_PAD - POOL_PAST:])
        new_wkv.append(_blocks_to_state(s_blk))
        x = merge_out(ya, yb, g, gates, x, sw["proj_pool"], sw["proj_rwkv"], sw["w_out"], i)
        x = ffn(x, norm_ffn[i], sw["w_ffn_gate"], sw["w_ffn_up"], sw["w_ffn_down"], i)
        x = ple_update(x, norm_ple[i], sw["w_ple_gate"], pp_tok, ps_tok, sw["w_ple_proj"], i)
    y_prompt = final_norm(x, norm_final, row0=0, rows=mp).reshape(bp, tp, D_MODEL)
    y_sample = final_norm(x, norm_final, row0=mp, rows=ms).reshape(bs, ts, D_MODEL)
    new_shift, new_pool, new_wkv = jnp.stack(new_shift), jnp.stack(new_pool), jnp.stack(new_wkv)
    return (y_prompt, y_sample, new_shift[:, :bp], new_pool[:, :bp], new_wkv[:, :bp],
            new_shift[:, bp:], new_pool[:, bp:], new_wkv[:, bp:])
```

```python
import functools

import jax
import jax.numpy as jnp
from jax import lax
from jax.experimental import pallas as pl
from jax.experimental.pallas import tpu as pltpu

F32 = jnp.float32
BF16 = jnp.bfloat16

D_MODEL = 2048
DEPTH = 4
PAST_LEN = 1024
PLE_DIM = 256
POOL_WINDOWS = (2, 4, 8, 16)
POOL_GROUPS = 4
POOL_WIDTH = D_MODEL // 2
POOL_GC = POOL_WIDTH // POOL_GROUPS
POOL_PAST = max(POOL_WINDOWS) - 1
RW_WIDTH = D_MODEL // 2
RW_HEAD = 64
RW_HEADS = RW_WIDTH // RW_HEAD
DECAY_LORA = 64
AAA_LORA = 64
GATE_LORA = 160
LORA = DECAY_LORA + AAA_LORA + GATE_LORA
RW_PROJ = 3 * RW_WIDTH + LORA
D_FF = 5632
NORM_EPS = 1e-6
GN_EPS = 64e-5

LANES = 128
POOL_PAD = 16
LORA_PAD = 512
LORA_GATE_PAD = 256
PAIRS = RW_WIDTH // LANES
CHUNK = 64
SUB = 16
COL_RKV = 0
COL_POOL = 3 * RW_WIDTH
COL_LORA = COL_POOL + POOL_WIDTH
MAIN_COLS = COL_LORA + LORA_PAD
GATE_COLS = 2 * D_MODEL

TN = 512
TM_WIDE = 1536
TM_ROW = 768
TM_MERGE = 384
TM_PLE = 512
VMEM_LIMIT = 56 * 1024 * 1024


def _params(sem):
    return pltpu.CompilerParams(dimension_semantics=sem, vmem_limit_bytes=VMEM_LIMIT)


def _rms(x, g):
    ms = jnp.mean(x * x, axis=-1, keepdims=True)
    return x * lax.rsqrt(ms + NORM_EPS) * g


def _wspec(k, n, layer, col_of):
    return pl.BlockSpec((None, k, n), lambda *idx: (layer, 0, col_of(*idx)))


def _in_proj_kernel(x_ref, g_ref, wt_ref, main_ref, gates_ref, h_scr):
    @pl.when(pl.program_id(1) == 0)
    def _():
        h_scr[...] = _rms(x_ref[...], g_ref[...]).astype(BF16)

    acc = lax.dot_general(h_scr[...], wt_ref[...], (((1,), (1,)), ((), ())), preferred_element_type=F32)
    main_ref[...] = acc
    gates_ref[...] = acc.astype(BF16)


def in_proj(x, g, wt, layer):
    m, k = x.shape
    tm = TM_WIDE
    main_tiles = MAIN_COLS // TN
    gate_tiles = GATE_COLS // TN
    return pl.pallas_call(
        _in_proj_kernel,
        out_shape=(jax.ShapeDtypeStruct((m, MAIN_COLS + TN), F32), jax.ShapeDtypeStruct((m, GATE_COLS + TN), BF16)),
        grid=(m // tm, main_tiles + gate_tiles),
        in_specs=[
            pl.BlockSpec((tm, k), lambda i, j: (i, 0)),
            pl.BlockSpec((1, k), lambda i, j: (0, 0)),
            pl.BlockSpec((None, TN, k), lambda i, j: (layer, j, 0)),
        ],
        out_specs=(pl.BlockSpec((tm, TN), lambda i, j: (i, jnp.minimum(j, main_tiles))),
                   pl.BlockSpec((tm, TN), lambda i, j: (i, jnp.where(j < main_tiles, gate_tiles, j - main_tiles)))),
        scratch_shapes=[pltpu.VMEM((tm, k), BF16)],
        compiler_params=_params(("parallel", "arbitrary")),
        name="in_proj",
    )(x, g.reshape(1, k), wt)


def _merge_out_kernel(ya_ref, yb_ref, g_ref, gates_ref, x_ref, wp_ref, wr_ref, wo_ref, o_ref, ybg_scr, m_scr):
    n = o_ref.shape[1]
    ybg_scr[...] = (yb_ref[...] * g_ref[...]).astype(BF16)
    for c in range(n // TN):
        sl = slice(c * TN, (c + 1) * TN)
        sr = slice(n + c * TN, n + (c + 1) * TN)
        a = jnp.dot(ya_ref[...], wp_ref[:, sl], preferred_element_type=F32)
        b = jnp.dot(ybg_scr[...], wr_ref[:, sl], preferred_element_type=F32)
        gate_a = jax.nn.sigmoid(gates_ref[:, sl].astype(F32))
        gate_b = jax.nn.sigmoid(gates_ref[:, sr].astype(F32))
        m_scr[:, sl] = (gate_a * a + gate_b * b).astype(BF16)
    for c in range(n // TN):
        sl = slice(c * TN, (c + 1) * TN)
        o_ref[:, sl] = x_ref[:, sl] + jnp.dot(m_scr[...], wo_ref[:, sl], preferred_element_type=F32)


def merge_out(ya, yb, g, gates, x, wp, wr, wo, layer):
    m, n = x.shape
    tm = TM_MERGE
    resident = lambda k_: pl.BlockSpec((None, k_, n), lambda i: (layer, 0, 0), pipeline_mode=pl.Buffered(1))
    return pl.pallas_call(
        _merge_out_kernel,
        out_shape=jax.ShapeDtypeStruct((m, n), F32),
        grid=(m // tm,),
        in_specs=[
            pl.BlockSpec((tm, POOL_WIDTH), lambda i: (i, 0)),
            pl.BlockSpec((tm, RW_WIDTH), lambda i: (i, 0)),
            pl.BlockSpec((tm, RW_WIDTH), lambda i: (i, 0)),
            pl.BlockSpec((tm, GATE_COLS), lambda i: (i, 0)),
            pl.BlockSpec((tm, n), lambda i: (i, 0)),
            resident(POOL_WIDTH), resident(RW_WIDTH), resident(n),
        ],
        out_specs=pl.BlockSpec((tm, n), lambda i: (i, 0)),
        scratch_shapes=[pltpu.VMEM((tm, RW_WIDTH), BF16), pltpu.VMEM((tm, n), BF16)],
        compiler_params=_params(("parallel",)),
        name="merge_out",
    )(ya, yb, g, gates, x, wp, wr, wo)


def _ffn_kernel(x_ref, g_ref, wg_ref, wu_ref, wd_ref, o_ref, h_scr):
    @pl.when(pl.program_id(1) == 0)
    def _():
        x = x_ref[...]
        h_scr[...] = _rms(x, g_ref[...]).astype(BF16)
        o_ref[...] = x

    h = h_scr[...]
    gate = jnp.dot(h, wg_ref[...], preferred_element_type=F32)
    up = jnp.dot(h, wu_ref[...], preferred_element_type=F32)
    act = (gate * jax.nn.sigmoid(gate) * up).astype(BF16)
    o_ref[...] += jnp.dot(act, wd_ref[...], preferred_element_type=F32)


def ffn(x, g, wg, wu, wd, layer):
    m, k = x.shape
    f = wg.shape[2]
    tm = TM_ROW
    return pl.pallas_call(
        _ffn_kernel,
        out_shape=jax.ShapeDtypeStruct((m, k), F32),
        grid=(m // tm, f // TN),
        in_specs=[
            pl.BlockSpec((tm, k), lambda i, j: (i, 0)),
            pl.BlockSpec((1, k), lambda i, j: (0, 0)),
            _wspec(k, TN, layer, lambda i, j: j),
            _wspec(k, TN, layer, lambda i, j: j),
            pl.BlockSpec((None, TN, k), lambda i, j: (layer, j, 0)),
        ],
        out_specs=pl.BlockSpec((tm, k), lambda i, j: (i, 0)),
        scratch_shapes=[pltpu.VMEM((tm, k), BF16)],
        compiler_params=_params(("parallel", "arbitrary")),
        name="ffn",
    )(x, g.reshape(1, k), wg, wu, wd)


def _ple_kernel(x_ref, g_ref, wg_ref, pp_ref, ps_ref, wp_ref, o_ref, h_scr, *, prompt_tiles):
    n = o_ref.shape[1]
    h_scr[...] = _rms(x_ref[...], g_ref[...]).astype(BF16)
    pb = jnp.where(pl.program_id(0) < prompt_tiles, pp_ref[...], ps_ref[...]).astype(BF16)
    for c in range(n // TN):
        sl = slice(c * TN, (c + 1) * TN)
        gate = jnp.dot(h_scr[...], wg_ref[:, sl], preferred_element_type=F32)
        emb = jnp.dot(pb, wp_ref[:, sl], preferred_element_type=F32)
        o_ref[:, sl] = x_ref[:, sl] + jax.nn.sigmoid(gate) * emb


def ple_update(x, g, wg, p_prompt, p_sample, wp, layer):
    m, k = x.shape
    n = wg.shape[2]
    kp = p_prompt.shape[2]
    tm = TM_PLE
    prompt_tiles = p_prompt.shape[1] // tm
    sample_tiles = p_sample.shape[1] // tm
    return pl.pallas_call(
        functools.partial(_ple_kernel, prompt_tiles=prompt_tiles),
        out_shape=jax.ShapeDtypeStruct((m, n), F32),
        grid=(m // tm,),
        in_specs=[
            pl.BlockSpec((tm, k), lambda i: (i, 0)),
            pl.BlockSpec((1, k), lambda i: (0, 0)),
            _wspec(k, n, layer, lambda i: 0),
            pl.BlockSpec((None, tm, kp), lambda i: (layer, jnp.minimum(i, prompt_tiles - 1), 0)),
            pl.BlockSpec((None, tm, kp), lambda i: (layer, jnp.clip(i - prompt_tiles, 0, sample_tiles - 1), 0)),
            _wspec(kp, n, layer, lambda i: 0),
        ],
        out_specs=pl.BlockSpec((tm, n), lambda i: (i, 0)),
        scratch_shapes=[pltpu.VMEM((tm, k), BF16)],
        compiler_params=_params(("parallel",)),
        name="ple_update",
    )(x, g.reshape(1, k), wg, p_prompt, p_sample, wp)


def _final_norm_kernel(x_ref, g_ref, o_ref):
    o_ref[...] = _rms(x_ref[...], g_ref[...])


def final_norm(x, g, *, row0, rows):
    k = x.shape[1]
    tm = 512
    rb0 = row0 // tm
    return pl.pallas_call(
        _final_norm_kernel,
        out_shape=jax.ShapeDtypeStruct((rows, k), F32),
        grid=(rows // tm,),
        in_specs=[pl.BlockSpec((tm, k), lambda i: (rb0 + i, 0)), pl.BlockSpec((1, k), lambda i: (0, 0))],
        out_specs=pl.BlockSpec((tm, k), lambda i: (i, 0)),
        compiler_params=_params(("parallel",)),
        name="final_norm",
    )(x, g.reshape(1, k))


def _shift_mix(cur, carry_row, mu):
    rolled = pltpu.roll(cur, 1, axis=0)
    row = lax.broadcasted_iota(jnp.int32, cur.shape, 0)
    prev = jnp.where(row == 0, carry_row, rolled)
    return cur + (prev - cur) * mu


def _softplus(x):
    return jnp.maximum(x, 0.0) + jnp.log1p(jnp.exp(-jnp.abs(x)))


def _mm(a, b):
    return jnp.dot(a.astype(BF16), b.astype(BF16), preferred_element_type=F32)


def _mm_nt(a, b):
    return lax.dot_general(a.astype(BF16), b.astype(BF16), (((1,), (1,)), ((), ())),
                           preferred_element_type=F32)


def _mm_tn(a, b):
    return lax.dot_general(a.astype(BF16), b.astype(BF16), (((0,), (0,)), ((), ())),
                           preferred_element_type=F32)


def _split2(x):
    hi = x.astype(BF16)
    lo = (x - hi.astype(F32)).astype(BF16)
    return hi, lo


def _segsum(x, seg):
    xb = x.astype(BF16)
    parts = [jnp.dot(xb[:, c * LANES:(c + 1) * LANES], seg, preferred_element_type=F32) for c in range(PAIRS)]
    return jnp.concatenate(parts, axis=-1)


def _stack2(x, m0):
    return jnp.concatenate([jnp.where(m0, x, 0.0), jnp.where(m0, 0.0, x)], axis=0)


def _seq_kernel(r_ref, k_ref, v_ref, l_ref, u_ref, sp_ref, spl_ref, past_ref, s0_ref,
                mu_ref, mul_ref, w0_ref, a0_ref, kkw_ref, ka_ref, wd_ref, wa_ref, wg_ref,
                pw_ref, ps_ref, gain_ref, bias_ref, rk_ref, seg_ref, tri_ref,
                ya_ref, yb_ref, g_ref, ns_ref, nsl_ref, np_ref, s_ref,
                carry, carry_l, carry_p, *, prompt_tiles, tiles_p, tiles_s):
    c = CHUNK
    i = pl.program_id(0)
    in_prompt = i < prompt_tiles
    t = jnp.where(in_prompt, i % tiles_p, (i - prompt_tiles) % tiles_s)
    pos0 = jnp.where(in_prompt, 0, PAST_LEN) + t * c

    @pl.when(t == 0)
    def _():
        carry[0:3, :] = sp_ref[0]
        carry_l[0:1, :] = spl_ref[0]
        carry_p[...] = past_ref[0]
        s_ref[...] = s0_ref[...]

    u = u_ref[...]
    ext = jnp.concatenate([carry_p[...], u], axis=0)
    carry_p[...] = ext[c:c + POOL_PAD]
    np_ref[0] = ext[c:c + POOL_PAD]
    pos = pos0 + lax.broadcasted_iota(jnp.int32, (c, 1), 0)
    outs = []
    for gi, win in enumerate(POOL_WINDOWS):
        sl = slice(gi * POOL_GC, (gi + 1) * POOL_GC)
        sw = ext[:, sl]
        sh = 1
        while sh < win:
            sw = sw + pltpu.roll(sw, sh, axis=0)
            sh *= 2
        cnt = jnp.minimum(pos + 1, win).astype(F32)
        dlt = sw[POOL_PAD:] / cnt - u[:, sl]
        outs.append(jnp.dot(dlt.astype(BF16), pw_ref[gi], preferred_element_type=F32))
    ya_ref[...] = (jnp.concatenate(outs, axis=-1) * ps_ref[...]).astype(BF16)

    r_in = r_ref[...]
    k_in = k_ref[...]
    v_in = v_ref[...]
    lo_in = l_ref[...]
    r = _shift_mix(r_in, carry[0:1, :], mu_ref[0:1, :])
    xk = _shift_mix(k_in, carry[1:2, :], mu_ref[1:2, :])
    v = _shift_mix(v_in, carry[2:3, :], mu_ref[2:3, :])
    xl = _shift_mix(lo_in, carry_l[0:1, :], mul_ref[...])
    last = jnp.concatenate([r_in[c - 1:c, :], k_in[c - 1:c, :], v_in[c - 1:c, :]], axis=0)
    carry[0:3, :] = last
    carry_l[0:1, :] = lo_in[c - 1:c, :]
    ns_ref[0] = last
    nsl_ref[0] = lo_in[c - 1:c, :]

    seg = seg_ref[...]
    xda = xl[:, 0:LANES]
    xg = xl[:, LANES:LANES + LORA_GATE_PAD]
    zd = jnp.dot(jnp.tanh(xda).astype(BF16), wd_ref[...], preferred_element_type=F32)
    logw = -jnp.exp(-_softplus(-(w0_ref[...] + zd)) - 0.5)
    za = jnp.dot(xda.astype(BF16), wa_ref[...], preferred_element_type=F32)
    a = jax.nn.sigmoid(a0_ref[...] + za)
    g_ref[...] = jnp.dot(jax.nn.sigmoid(xg).astype(BF16), wg_ref[...], preferred_element_type=F32)
    kkr = xk * kkw_ref[...]
    kk = kkr / jnp.maximum(jnp.sqrt(_segsum(kkr * kkr, seg)), 1e-12)
    k = xk * (1.0 + (a - 1.0) * ka_ref[...])
    bb = kk * a

    hi, lo = _split2(logw)
    tri = tri_ref[...]
    linc = jnp.dot(tri, hi, preferred_element_type=F32) + jnp.dot(tri, lo, preferred_element_type=F32)
    lexc = linc - logw
    ltot = linc[c - 1:c, :]
    e_ninc = jnp.exp(-linc)
    kt = kk * jnp.exp(lexc)
    rt = r * jnp.exp(linc)
    bh = bb * e_ninc
    kh = k * e_ninc
    e_rem = jnp.exp(ltot - linc)
    bbar = bb * e_rem
    kbar = k * e_rem
    gtot = jnp.exp(ltot)

    ri = lax.broadcasted_iota(jnp.int32, (2 * c, 2 * c), 0)
    ci = lax.broadcasted_iota(jnp.int32, (2 * c, 2 * c), 1)
    strict = (ci % c) < (ri % c)
    incl = (ci % c) <= (ri % c)
    blk = (ri // SUB) == (ci // SUB)
    eye = ri == ci
    m0 = lax.broadcasted_iota(jnp.int32, (c, LANES), 1) < RW_HEAD

    pairs = range(PAIRS)
    lanes = [slice(p * LANES, (p + 1) * LANES) for p in pairs]
    each = lambda fn, *cols: [fn(*args) for args in zip(*cols)]
    stacked = lambda x: [_stack2(x[:, sl], m0) for sl in lanes]
    kt2, rt2, bh2, kh2, bbar2, kbar2, v2 = (stacked(x) for x in (kt, rt, bh, kh, bbar, kbar, v))

    aa = each(lambda a_, b_, c_, d_: _mm_nt(jnp.concatenate([a_, b_], axis=0), jnp.concatenate([c_, d_], axis=0)),
              kt2, rt2, bh2, kh2)
    lp = [jnp.where(strict, x[0:2 * c, 0:2 * c], 0.0) for x in aa]
    akp = [jnp.where(strict, x[0:2 * c, 2 * c:4 * c], 0.0) for x in aa]
    arb = [jnp.where(incl, x[2 * c:4 * c, 0:2 * c], 0.0) for x in aa]
    ark = [jnp.where(incl, x[2 * c:4 * c, 2 * c:4 * c], 0.0) for x in aa]

    ld = [jnp.where(blk, x, 0.0) for x in lp]
    lo_ = each(lambda x, y_: x - y_, lp, ld)
    p2 = each(_mm, ld, ld)
    p4 = each(_mm, p2, p2)
    p8 = each(_mm, p4, p4)
    e1 = each(lambda a_, b_, m: a_ - b_ - m, p2, ld, each(_mm, ld, p2))
    e2 = each(lambda a_, b_, m: a_ + b_ + m, e1, p4, each(_mm, e1, p4))
    dm = each(lambda a_, b_, m: a_ + b_ + m, e2, p8, each(_mm, e2, p8))
    n1 = each(lambda a_, m: a_ + m, lo_, each(_mm, dm, lo_))
    n2 = each(_mm, n1, n1)
    f = each(lambda a_, b_, m: a_ - b_ - m, n2, n1, each(_mm, n1, n2))
    tm = each(lambda a_, b_, m: a_ + b_ + m, f, dm, each(_mm, f, dm))

    akv = each(_mm, akp, v2)
    uu = each(lambda a_, b_: jnp.concatenate([a_, b_], axis=1), kt2, akv)
    pq = each(lambda a_, m: a_ + m, uu, each(_mm, tm, uu))
    rbpq = each(_mm, arb, pq)
    ry = each(lambda a_, m: a_ - m[:, 0:LANES], rt2, rbpq)
    y0 = each(lambda m, n_: m - n_[:, LANES:2 * LANES], each(_mm, ark, v2), rbpq)
    btpq = each(_mm_tn, bbar2, pq)
    g2 = [jnp.where(eye, gtot[:, sl], 0.0) - m[:, 0:LANES] for sl, m in zip(lanes, btpq)]
    h2 = each(lambda m, n_: m - n_[:, LANES:2 * LANES], each(_mm_tn, kbar2, v2), btpq)

    st = [s_ref[0, p] for p in pairs]
    y2 = each(lambda a_, b_, o: _mm(a_, b_) + o, ry, st, y0)
    s_new = each(lambda a_, b_, o: _mm(a_, b_) + o, g2, st, h2)
    for p in pairs:
        s_ref[0, p] = s_new[p]
    y = jnp.concatenate([x[0:c] + x[c:2 * c] for x in y2], axis=-1)

    inv_n = 1.0 / RW_HEAD
    mean = _segsum(y, seg) * inv_n
    yc = y - mean
    var = _segsum(yc * yc, seg) * inv_n
    yn = yc * lax.rsqrt(var + GN_EPS) * gain_ref[...] + bias_ref[...]
    bonus = _segsum(r * k * rk_ref[...], seg)
    yb_ref[...] = yn + bonus * v


def seq_mixers(proj, shift_rkv, shift_lora, pool_past, s0_blk, lw, *, prompt_batch, prompt_len, sample_len):
    c = CHUNK
    n_seq = shift_rkv.shape[0]
    tiles_p = prompt_len // c
    tiles_s = sample_len // c
    prompt_tiles = prompt_batch * tiles_p
    m = proj.shape[0]
    n_tiles = m // c
    seq = lambda i: jnp.where(i < prompt_tiles, i // tiles_p, prompt_batch + (i - prompt_tiles) // tiles_s)
    col = lambda width, cb: pl.BlockSpec((c, width), lambda i: (i, cb))
    per_seq = lambda *shape: pl.BlockSpec((1,) + shape, lambda i: (seq(i),) + tuple(0 for _ in shape))
    full = lambda *shape: pl.BlockSpec(shape, lambda i: tuple(0 for _ in shape))
    tri = (jnp.arange(c)[:, None] >= jnp.arange(c)[None, :]).astype(BF16)
    row = lambda v: v.reshape(1, -1)
    return pl.pallas_call(
        functools.partial(_seq_kernel, prompt_tiles=prompt_tiles, tiles_p=tiles_p, tiles_s=tiles_s),
        out_shape=(jax.ShapeDtypeStruct((m, POOL_WIDTH), BF16),
                   jax.ShapeDtypeStruct((m, RW_WIDTH), F32),
                   jax.ShapeDtypeStruct((m, RW_WIDTH), F32),
                   jax.ShapeDtypeStruct((n_seq, 3, RW_WIDTH), F32),
                   jax.ShapeDtypeStruct((n_seq, 1, LORA_PAD), F32),
                   jax.ShapeDtypeStruct((n_seq, POOL_PAD, POOL_WIDTH), F32),
                   jax.ShapeDtypeStruct((n_seq, PAIRS, LANES, LANES), F32)),
        grid=(n_tiles,),
        in_specs=[
            col(RW_WIDTH, 0), col(RW_WIDTH, 1), col(RW_WIDTH, 2), col(LORA_PAD, COL_LORA // LORA_PAD),
            col(POOL_WIDTH, COL_POOL // POOL_WIDTH),
            per_seq(3, RW_WIDTH), per_seq(1, LORA_PAD), per_seq(POOL_PAD, POOL_WIDTH), per_seq(PAIRS, LANES, LANES),
            full(3, RW_WIDTH), full(1, LORA_PAD),
            full(1, RW_WIDTH), full(1, RW_WIDTH), full(1, RW_WIDTH), full(1, RW_WIDTH),
            full(LANES, RW_WIDTH), full(LANES, RW_WIDTH), full(LORA_GATE_PAD, RW_WIDTH),
            full(POOL_GROUPS, POOL_GC, POOL_GC), full(1, POOL_WIDTH),
            full(1, RW_WIDTH), full(1, RW_WIDTH), full(1, RW_WIDTH),
            full(LANES, LANES), full(c, c),
        ],
        out_specs=(col(POOL_WIDTH, 0), col(RW_WIDTH, 0), col(RW_WIDTH, 0),
                   per_seq(3, RW_WIDTH), per_seq(1, LORA_PAD), per_seq(POOL_PAD, POOL_WIDTH),
                   per_seq(PAIRS, LANES, LANES)),
        scratch_shapes=[pltpu.VMEM((8, RW_WIDTH), F32), pltpu.VMEM((8, LORA_PAD), F32),
                        pltpu.VMEM((POOL_PAD, POOL_WIDTH), F32)],
        compiler_params=_params(("arbitrary",)),
        name="seq_mixers",
    )(proj, proj, proj, proj, proj, shift_rkv, shift_lora, pool_past, s0_blk,
      lw["mu_rkv"], lw["mu_lora"], lw["w0"], lw["a0"], lw["k_k"], lw["k_a"], lw["wd"], lw["wa"], lw["wg"],
      lw["pool_w"], row(lw["pool_scale"]), row(lw["gn_gain"]), row(lw["gn_bias"]), row(lw["r_k"]), lw["seg"], tri)


def _state_to_blocks(s0):
    b = s0.shape[0]
    st = jnp.swapaxes(s0, -1, -2).reshape(b, PAIRS, 2, RW_HEAD, RW_HEAD)
    z = jnp.zeros_like(st[:, :, 0])
    top = jnp.concatenate([st[:, :, 0], z], axis=-1)
    bottom = jnp.concatenate([z, st[:, :, 1]], axis=-1)
    return jnp.concatenate([top, bottom], axis=-2)


def _blocks_to_state(sb):
    b = sb.shape[0]
    d = jnp.stack([sb[:, :, :RW_HEAD, :RW_HEAD], sb[:, :, RW_HEAD:, RW_HEAD:]], axis=2)
    return jnp.swapaxes(d.reshape(b, RW_HEADS, RW_HEAD, RW_HEAD), -1, -2)


def _stacked_weights(w_in, pool_w, w_decay_up, w_aaa_up, w_gate_up, proj_pool, proj_rwkv, w_out,
                     w_ffn_gate, w_ffn_up, w_ffn_down, w_ple_gate, w_ple_proj):
    c_rw = POOL_WIDTH
    c_lora = POOL_WIDTH + 3 * RW_WIDTH
    c_gp = POOL_WIDTH + RW_PROJ
    w_t = jnp.swapaxes(w_in, 1, 2)
    w_in_t = jnp.concatenate([
        w_t[:, c_rw:c_lora], w_t[:, :c_rw], w_t[:, c_lora:c_gp],
        jnp.zeros((DEPTH, LORA_PAD - LORA, D_MODEL), F32), w_t[:, c_gp:]], axis=1).astype(BF16)
    zpad = lambda rows: jnp.zeros((DEPTH, rows, RW_WIDTH), F32)
    return dict(
        w_in=w_in_t, pool_w=pool_w.astype(BF16),
        wd=jnp.concatenate([w_decay_up, zpad(LANES - DECAY_LORA)], axis=1).astype(BF16),
        wa=jnp.concatenate([zpad(DECAY_LORA), w_aaa_up], axis=1).astype(BF16),
        wg=jnp.concatenate([w_gate_up, zpad(LORA_GATE_PAD - GATE_LORA)], axis=1).astype(BF16),
        proj_pool=proj_pool.astype(BF16), proj_rwkv=proj_rwkv.astype(BF16), w_out=w_out.astype(BF16),
        w_ffn_gate=w_ffn_gate.astype(BF16), w_ffn_up=w_ffn_up.astype(BF16), w_ffn_down=w_ffn_down.astype(BF16),
        w_ple_gate=w_ple_gate.astype(BF16), w_ple_proj=w_ple_proj.astype(BF16))


def kernel(x_prompt, x_sample, state_shift, state_pool, state_wkv, p_prompt, p_sample, norm_mix, w_in, mu_shift, pool_w, pool_scale, w0, w_decay_up, a0, w_aaa_up, w_gate_up, k_k, k_a, r_k, gn_gain, gn_bias, proj_pool, proj_rwkv, w_out, norm_ffn, w_ffn_gate, w_ffn_up, w_ffn_down, norm_ple, w_ple_gate, w_ple_proj, norm_final):
    bp, tp, _ = x_prompt.shape
    bs, ts, _ = x_sample.shape
    mp = bp * tp
    ms = bs * ts
    x = jnp.concatenate([x_prompt.reshape(mp, D_MODEL), x_sample.reshape(ms, D_MODEL)], axis=0)
    pp_tok = p_prompt.reshape(DEPTH, mp, PLE_DIM)
    ps_tok = p_sample.reshape(DEPTH, ms, PLE_DIM)
    sw = _stacked_weights(w_in, pool_w, w_decay_up, w_aaa_up, w_gate_up, proj_pool, proj_rwkv, w_out,
                          w_ffn_gate, w_ffn_up, w_ffn_down, w_ple_gate, w_ple_proj)
    lane = jnp.arange(LANES) // RW_HEAD
    seg = (lane[:, None] == lane[None, :]).astype(BF16)
    lead = lambda a: jnp.concatenate([jnp.zeros((DEPTH, bp) + a.shape[2:], a.dtype), a], axis=1)
    shift_all = lead(state_shift)
    shift_rkv = shift_all[:, :, :3 * RW_WIDTH].reshape(DEPTH, bp + bs, 3, RW_WIDTH)
    shift_lora = jnp.pad(shift_all[:, :, 3 * RW_WIDTH:], ((0, 0), (0, 0), (0, LORA_PAD - LORA)))
    shift_lora = shift_lora.reshape(DEPTH, bp + bs, 1, LORA_PAD)
    pool_all = jnp.pad(lead(state_pool), ((0, 0), (0, 0), (POOL_PAD - POOL_PAST, 0), (0, 0)))
    wkv_all = lead(state_wkv)
    row = lambda v: v.reshape(1, -1)
    new_shift, new_pool, new_wkv = [], [], []
    for i in range(DEPTH):
        mu = mu_shift[i]
        lw = dict(
            mu_rkv=mu[:3 * RW_WIDTH].reshape(3, RW_WIDTH),
            mu_lora=jnp.pad(mu[3 * RW_WIDTH:], (0, LORA_PAD - LORA)).reshape(1, LORA_PAD),
            pool_w=sw["pool_w"][i], pool_scale=pool_scale[i],
            w0=row(w0[i]), a0=row(a0[i]), k_k=row(k_k[i]), k_a=row(k_a[i]),
            wd=sw["wd"][i], wa=sw["wa"][i], wg=sw["wg"][i], seg=seg,
            r_k=r_k[i].reshape(RW_WIDTH), gn_gain=gn_gain[i], gn_bias=gn_bias[i])
        proj, gates = in_proj(x, norm_mix[i], sw["w_in"], i)
        ya, yb, g, ns_rkv, ns_lora, pool_rows, s_blk = seq_mixers(
            proj, shift_rkv[i], shift_lora[i], pool_all[i], _state_to_blocks(wkv_all[i]), lw,
            prompt_batch=bp, prompt_len=tp, sample_len=ts)
        new_shift.append(jnp.concatenate([ns_rkv.reshape(bp + bs, 3 * RW_WIDTH), ns_lora[:, 0, :LORA]], axis=1))
        new_pool.append(pool_rows[:, POOL_PAD - POOL_PAST:])
        new_wkv.append(_blocks_to_state(s_blk))
        x = merge_out(ya, yb, g, gates, x, sw["proj_pool"], sw["proj_rwkv"], sw["w_out"], i)
        x = ffn(x, norm_ffn[i], sw["w_ffn_gate"], sw["w_ffn_up"], sw["w_ffn_down"], i)
        x = ple_update(x, norm_ple[i], sw["w_ple_gate"], pp_tok, ps_tok, sw["w_ple_proj"], i)
    y_prompt = final_norm(x, norm_final, row0=0, rows=mp).reshape(bp, tp, D_MODEL)
    y_sample = final_norm(x, norm_final, row0=mp, rows=ms).reshape(bs, ts, D_MODEL)
    new_shift, new_pool, new_wkv = jnp.stack(new_shift), jnp.stack(new_pool), jnp.stack(new_wkv)
    return (y_prompt, y_sample, new_shift[:, :bp], new_pool[:, :bp], new_wkv[:, :bp],
            new_shift[:, bp:], new_pool[:, bp:], new_wkv[:, bp:])
```

```python
import functools

import jax
import jax.numpy as jnp
from jax import lax
from jax.experimental import pallas as pl
from jax.experimental.pallas import tpu as pltpu

F32 = jnp.float32
BF16 = jnp.bfloat16

D_MODEL = 2048
DEPTH = 4
PAST_LEN = 1024
PLE_DIM = 256
POOL_WINDOWS = (2, 4, 8, 16)
POOL_GROUPS = 4
POOL_WIDTH = D_MODEL // 2
POOL_GC = POOL_WIDTH // POOL_GROUPS
POOL_PAST = max(POOL_WINDOWS) - 1
RW_WIDTH = D_MODEL // 2
RW_HEAD = 64
RW_HEADS = RW_WIDTH // RW_HEAD
DECAY_LORA = 64
AAA_LORA = 64
GATE_LORA = 160
LORA = DECAY_LORA + AAA_LORA + GATE_LORA
RW_PROJ = 3 * RW_WIDTH + LORA
D_FF = 5632
NORM_EPS = 1e-6
GN_EPS = 64e-5

LANES = 128
POOL_PAD = 16
LORA_PAD = 512
LORA_GATE_PAD = 256
PAIRS = RW_WIDTH // LANES
CHUNK = 64
SUB = 16
COL_RKV = 0
COL_POOL = 3 * RW_WIDTH
COL_LORA = COL_POOL + POOL_WIDTH
MAIN_COLS = COL_LORA + LORA_PAD
GATE_COLS = 2 * D_MODEL

TN = 512
TM_WIDE = 1536
TM_ROW = 768
TM_MERGE = 384
TM_PLE = 512
VMEM_LIMIT = 56 * 1024 * 1024


def _params(sem):
    return pltpu.CompilerParams(dimension_semantics=sem, vmem_limit_bytes=VMEM_LIMIT)


def _rms(x, g):
    ms = jnp.mean(x * x, axis=-1, keepdims=True)
    return x * lax.rsqrt(ms + NORM_EPS) * g


def _wspec(k, n, layer, col_of):
    return pl.BlockSpec((None, k, n), lambda *idx: (layer, 0, col_of(*idx)))


def _in_proj_kernel(x_ref, g_ref, wt_ref, main_ref, gates_ref, h_scr):
    @pl.when(pl.program_id(1) == 0)
    def _():
        h_scr[...] = _rms(x_ref[...], g_ref[...]).astype(BF16)

    acc = lax.dot_general(h_scr[...], wt_ref[...], (((1,), (1,)), ((), ())), preferred_element_type=F32)
    main_ref[...] = acc
    gates_ref[...] = acc.astype(BF16)


def in_proj(x, g, wt, layer):
    m, k = x.shape
    tm = TM_WIDE
    main_tiles = MAIN_COLS // TN
    gate_tiles = GATE_COLS // TN
    return pl.pallas_call(
        _in_proj_kernel,
        out_shape=(jax.ShapeDtypeStruct((m, MAIN_COLS + TN), F32), jax.ShapeDtypeStruct((m, GATE_COLS + TN), BF16)),
        grid=(m // tm, main_tiles + gate_tiles),
        in_specs=[
            pl.BlockSpec((tm, k), lambda i, j: (i, 0)),
            pl.BlockSpec((1, k), lambda i, j: (0, 0)),
            pl.BlockSpec((None, TN, k), lambda i, j: (layer, j, 0)),
        ],
        out_specs=(pl.BlockSpec((tm, TN), lambda i, j: (i, jnp.minimum(j, main_tiles))),
                   pl.BlockSpec((tm, TN), lambda i, j: (i, jnp.where(j < main_tiles, gate_tiles, j - main_tiles)))),
        scratch_shapes=[pltpu.VMEM((tm, k), BF16)],
        compiler_params=_params(("parallel", "arbitrary")),
        name="in_proj",
    )(x, g.reshape(1, k), wt)


def _merge_out_kernel(ya_ref, yb_ref, g_ref, gates_ref, x_ref, wp_ref, wr_ref, wo_ref, o_ref, ybg_scr, m_scr):
    n = o_ref.shape[1]
    ybg_scr[...] = (yb_ref[...] * g_ref[...]).astype(BF16)
    for c in range(n // TN):
        sl = slice(c * TN, (c + 1) * TN)
        sr = slice(n + c * TN, n + (c + 1) * TN)
        a = jnp.dot(ya_ref[...], wp_ref[:, sl], preferred_element_type=F32)
        b = jnp.dot(ybg_scr[...], wr_ref[:, sl], preferred_element_type=F32)
        gate_a = jax.nn.sigmoid(gates_ref[:, sl].astype(F32))
        gate_b = jax.nn.sigmoid(gates_ref[:, sr].astype(F32))
        m_scr[:, sl] = (gate_a * a + gate_b * b).astype(BF16)
    for c in range(n // TN):
        sl = slice(c * TN, (c + 1) * TN)
        o_ref[:, sl] = x_ref[:, sl] + jnp.dot(m_scr[...], wo_ref[:, sl], preferred_element_type=F32)


def merge_out(ya, yb, g, gates, x, wp, wr, wo, layer):
    m, n = x.shape
    tm = TM_MERGE
    resident = lambda k_: pl.BlockSpec((None, k_, n), lambda i: (layer, 0, 0), pipeline_mode=pl.Buffered(1))
    return pl.pallas_call(
        _merge_out_kernel,
        out_shape=jax.ShapeDtypeStruct((m, n), F32),
        grid=(m // tm,),
        in_specs=[
            pl.BlockSpec((tm, POOL_WIDTH), lambda i: (i, 0)),
            pl.BlockSpec((tm, RW_WIDTH), lambda i: (i, 0)),
            pl.BlockSpec((tm, RW_WIDTH), lambda i: (i, 0)),
            pl.BlockSpec((tm, GATE_COLS), lambda i: (i, 0)),
            pl.BlockSpec((tm, n), lambda i: (i, 0)),
            resident(POOL_WIDTH), resident(RW_WIDTH), resident(n),
        ],
        out_specs=pl.BlockSpec((tm, n), lambda i: (i, 0)),
        scratch_shapes=[pltpu.VMEM((tm, RW_WIDTH), BF16), pltpu.VMEM((tm, n), BF16)],
        compiler_params=_params(("parallel",)),
        name="merge_out",
    )(ya, yb, g, gates, x, wp, wr, wo)


def _ffn_kernel(x_ref, g_ref, wg_ref, wu_ref, wd_ref, o_ref, h_scr):
    @pl.when(pl.program_id(1) == 0)
    def _():
        x = x_ref[...]
        h_scr[...] = _rms(x, g_ref[...]).astype(BF16)
        o_ref[...] = x

    h = h_scr[...]
    gate = jnp.dot(h, wg_ref[...], preferred_element_type=F32)
    up = jnp.dot(h, wu_ref[...], preferred_element_type=F32)
    act = (gate * jax.nn.sigmoid(gate) * up).astype(BF16)
    o_ref[...] += jnp.dot(act, wd_ref[...], preferred_element_type=F32)


def ffn(x, g, wg, wu, wd, layer):
    m, k = x.shape
    f = wg.shape[2]
    tm = TM_ROW
    return pl.pallas_call(
        _ffn_kernel,
        out_shape=jax.ShapeDtypeStruct((m, k), F32),
        grid=(m // tm, f // TN),
        in_specs=[
            pl.BlockSpec((tm, k), lambda i, j: (i, 0)),
            pl.BlockSpec((1, k), lambda i, j: (0, 0)),
            _wspec(k, TN, layer, lambda i, j: j),
            _wspec(k, TN, layer, lambda i, j: j),
            pl.BlockSpec((None, TN, k), lambda i, j: (layer, j, 0)),
        ],
        out_specs=pl.BlockSpec((tm, k), lambda i, j: (i, 0)),
        scratch_shapes=[pltpu.VMEM((tm, k), BF16)],
        compiler_params=_params(("parallel", "arbitrary")),
        name="ffn",
    )(x, g.reshape(1, k), wg, wu, wd)


def _ple_kernel(x_ref, g_ref, wg_ref, pp_ref, ps_ref, wp_ref, o_ref, h_scr, *, prompt_tiles):
    n = o_ref.shape[1]
    h_scr[...] = _rms(x_ref[...], g_ref[...]).astype(BF16)
    pb = jnp.where(pl.program_id(0) < prompt_tiles, pp_ref[...], ps_ref[...]).astype(BF16)
    for c in range(n // TN):
        sl = slice(c * TN, (c + 1) * TN)
        gate = jnp.dot(h_scr[...], wg_ref[:, sl], preferred_element_type=F32)
        emb = jnp.dot(pb, wp_ref[:, sl], preferred_element_type=F32)
        o_ref[:, sl] = x_ref[:, sl] + jax.nn.sigmoid(gate) * emb


def ple_update(x, g, wg, p_prompt, p_sample, wp, layer):
    m, k = x.shape
    n = wg.shape[2]
    kp = p_prompt.shape[2]
    tm = TM_PLE
    prompt_tiles = p_prompt.shape[1] // tm
    sample_tiles = p_sample.shape[1] // tm
    return pl.pallas_call(
        functools.partial(_ple_kernel, prompt_tiles=prompt_tiles),
        out_shape=jax.ShapeDtypeStruct((m, n), F32),
        grid=(m // tm,),
        in_specs=[
            pl.BlockSpec((tm, k), lambda i: (i, 0)),
            pl.BlockSpec((1, k), lambda i: (0, 0)),
            _wspec(k, n, layer, lambda i: 0),
            pl.BlockSpec((None, tm, kp), lambda i: (layer, jnp.minimum(i, prompt_tiles - 1), 0)),
            pl.BlockSpec((None, tm, kp), lambda i: (layer, jnp.clip(i - prompt_tiles, 0, sample_tiles - 1), 0)),
            _wspec(kp, n, layer, lambda i: 0),
        ],
        out_specs=pl.BlockSpec((tm, n), lambda i: (i, 0)),
        scratch_shapes=[pltpu.VMEM((tm, k), BF16)],
        compiler_params=_params(("parallel",)),
        name="ple_update",
    )(x, g.reshape(1, k), wg, p_prompt, p_sample, wp)


def _ple_final_kernel(x_ref, g_ref, wg_ref, pp_ref, ps_ref, wp_ref, gf_ref, yp_ref, ys_ref, h_scr, x_scr,
                      *, prompt_tiles):
    n = x_scr.shape[1]
    i = pl.program_id(0)
    h_scr[...] = _rms(x_ref[...], g_ref[...]).astype(BF16)
    pb = jnp.where(i < prompt_tiles, pp_ref[...], ps_ref[...]).astype(BF16)
    for c in range(n // TN):
        sl = slice(c * TN, (c + 1) * TN)
        gate = jnp.dot(h_scr[...], wg_ref[:, sl], preferred_element_type=F32)
        emb = jnp.dot(pb, wp_ref[:, sl], preferred_element_type=F32)
        x_scr[:, sl] = x_ref[:, sl] + jax.nn.sigmoid(gate) * emb
    y = _rms(x_scr[...], gf_ref[...])
    ys_ref[...] = y

    @pl.when(i < prompt_tiles)
    def _():
        yp_ref[...] = y


def ple_final(x, g, wg, p_prompt, p_sample, wp, g_final, layer):
    m, k = x.shape
    n = wg.shape[2]
    kp = p_prompt.shape[2]
    tm = TM_PLE
    prompt_tiles = p_prompt.shape[1] // tm
    sample_tiles = p_sample.shape[1] // tm
    sample_blk = lambda i: jnp.clip(i - prompt_tiles, 0, sample_tiles - 1)
    return pl.pallas_call(
        functools.partial(_ple_final_kernel, prompt_tiles=prompt_tiles),
        out_shape=(jax.ShapeDtypeStruct((p_prompt.shape[1], n), F32), jax.ShapeDtypeStruct((p_sample.shape[1], n), F32)),
        grid=(m // tm,),
        in_specs=[
            pl.BlockSpec((tm, k), lambda i: (i, 0)),
            pl.BlockSpec((1, k), lambda i: (0, 0)),
            _wspec(k, n, layer, lambda i: 0),
            pl.BlockSpec((None, tm, kp), lambda i: (layer, jnp.minimum(i, prompt_tiles - 1), 0)),
            pl.BlockSpec((None, tm, kp), lambda i: (layer, sample_blk(i), 0)),
            _wspec(kp, n, layer, lambda i: 0),
            pl.BlockSpec((1, n), lambda i: (0, 0)),
        ],
        out_specs=(pl.BlockSpec((tm, n), lambda i: (jnp.minimum(i, prompt_tiles - 1), 0)),
                   pl.BlockSpec((tm, n), lambda i: (sample_blk(i), 0))),
        scratch_shapes=[pltpu.VMEM((tm, k), BF16), pltpu.VMEM((tm, n), F32)],
        compiler_params=_params(("arbitrary",)),
        name="ple_final",
    )(x, g.reshape(1, k), wg, p_prompt, p_sample, wp, g_final.reshape(1, n))


def _shift_mix(cur, carry_row, mu):
    rolled = pltpu.roll(cur, 1, axis=0)
    row = lax.broadcasted_iota(jnp.int32, cur.shape, 0)
    prev = jnp.where(row == 0, carry_row, rolled)
    return cur + (prev - cur) * mu


def _softplus(x):
    return jnp.maximum(x, 0.0) + jnp.log1p(jnp.exp(-jnp.abs(x)))


def _mm(a, b):
    return jnp.dot(a.astype(BF16), b.astype(BF16), preferred_element_type=F32)


def _mm_nt(a, b):
    return lax.dot_general(a.astype(BF16), b.astype(BF16), (((1,), (1,)), ((), ())),
                           preferred_element_type=F32)


def _mm_tn(a, b):
    return lax.dot_general(a.astype(BF16), b.astype(BF16), (((0,), (0,)), ((), ())),
                           preferred_element_type=F32)


def _split2(x):
    hi = x.astype(BF16)
    lo = (x - hi.astype(F32)).astype(BF16)
    return hi, lo


def _segsum(x, seg):
    xb = x.astype(BF16)
    parts = [jnp.dot(xb[:, c * LANES:(c + 1) * LANES], seg, preferred_element_type=F32) for c in range(PAIRS)]
    return jnp.concatenate(parts, axis=-1)


def _stack2(x, m0):
    return jnp.concatenate([jnp.where(m0, x, 0.0), jnp.where(m0, 0.0, x)], axis=0)


def _seq_kernel(r_ref, k_ref, v_ref, l_ref, u_ref, sp_ref, spl_ref, past_ref, s0_ref,
                mu_ref, mul_ref, w0_ref, a0_ref, kkw_ref, ka_ref, wd_ref, wa_ref, wg_ref,
                pw_ref, ps_ref, gain_ref, bias_ref, rk_ref, seg_ref, tri_ref,
                ya_ref, yb_ref, g_ref, ns_ref, nsl_ref, np_ref, s_ref,
                carry, carry_l, carry_p, *, prompt_tiles, tiles_p, tiles_s):
    c = CHUNK
    i = pl.program_id(0)
    in_prompt = i < prompt_tiles
    t = jnp.where(in_prompt, i % tiles_p, (i - prompt_tiles) % tiles_s)
    pos0 = jnp.where(in_prompt, 0, PAST_LEN) + t * c

    @pl.when(t == 0)
    def _():
        carry[0:3, :] = sp_ref[0]
        carry_l[0:1, :] = spl_ref[0]
        carry_p[...] = past_ref[0]
        s_ref[...] = s0_ref[...]

    u = u_ref[...]
    ext = jnp.concatenate([carry_p[...], u], axis=0)
    carry_p[...] = ext[c:c + POOL_PAD]
    np_ref[0] = ext[c:c + POOL_PAD]
    pos = pos0 + lax.broadcasted_iota(jnp.int32, (c, 1), 0)
    outs = []
    for gi, win in enumerate(POOL_WINDOWS):
        sl = slice(gi * POOL_GC, (gi + 1) * POOL_GC)
        sw = ext[:, sl]
        sh = 1
        while sh < win:
            sw = sw + pltpu.roll(sw, sh, axis=0)
            sh *= 2
        cnt = jnp.minimum(pos + 1, win).astype(F32)
        dlt = sw[POOL_PAD:] / cnt - u[:, sl]
        outs.append(jnp.dot(dlt.astype(BF16), pw_ref[gi], preferred_element_type=F32))
    ya_ref[...] = (jnp.concatenate(outs, axis=-1) * ps_ref[...]).astype(BF16)

    r_in = r_ref[...]
    k_in = k_ref[...]
    v_in = v_ref[...]
    lo_in = l_ref[...]
    r = _shift_mix(r_in, carry[0:1, :], mu_ref[0:1, :])
    xk = _shift_mix(k_in, carry[1:2, :], mu_ref[1:2, :])
    v = _shift_mix(v_in, carry[2:3, :], mu_ref[2:3, :])
    xl = _shift_mix(lo_in, carry_l[0:1, :], mul_ref[...])
    last = jnp.concatenate([r_in[c - 1:c, :], k_in[c - 1:c, :], v_in[c - 1:c, :]], axis=0)
    carry[0:3, :] = last
    carry_l[0:1, :] = lo_in[c - 1:c, :]
    ns_ref[0] = last
    nsl_ref[0] = lo_in[c - 1:c, :]

    seg = seg_ref[...]
    xda = xl[:, 0:LANES]
    xg = xl[:, LANES:LANES + LORA_GATE_PAD]
    zd = jnp.dot(jnp.tanh(xda).astype(BF16), wd_ref[...], preferred_element_type=F32)
    logw = -jnp.exp(-_softplus(-(w0_ref[...] + zd)) - 0.5)
    za = jnp.dot(xda.astype(BF16), wa_ref[...], preferred_element_type=F32)
    a = jax.nn.sigmoid(a0_ref[...] + za)
    g_ref[...] = jnp.dot(jax.nn.sigmoid(xg).astype(BF16), wg_ref[...], preferred_element_type=F32)
    kkr = xk * kkw_ref[...]
    kk = kkr / jnp.maximum(jnp.sqrt(_segsum(kkr * kkr, seg)), 1e-12)
    k = xk * (1.0 + (a - 1.0) * ka_ref[...])
    bb = kk * a

    hi, lo = _split2(logw)
    tri = tri_ref[...]
    linc = jnp.dot(tri, hi, preferred_element_type=F32) + jnp.dot(tri, lo, preferred_element_type=F32)
    lexc = linc - logw
    ltot = linc[c - 1:c, :]
    e_ninc = jnp.exp(-linc)
    kt = kk * jnp.exp(lexc)
    rt = r * jnp.exp(linc)
    bh = bb * e_ninc
    kh = k * e_ninc
    e_rem = jnp.exp(ltot - linc)
    bbar = bb * e_rem
    kbar = k * e_rem
    gtot = jnp.exp(ltot)

    ri = lax.broadcasted_iota(jnp.int32, (2 * c, 2 * c), 0)
    ci = lax.broadcasted_iota(jnp.int32, (2 * c, 2 * c), 1)
    strict = (ci % c) < (ri % c)
    incl = (ci % c) <= (ri % c)
    blk = (ri // SUB) == (ci // SUB)
    eye = ri == ci
    m0 = lax.broadcasted_iota(jnp.int32, (c, LANES), 1) < RW_HEAD

    pairs = range(PAIRS)
    lanes = [slice(p * LANES, (p + 1) * LANES) for p in pairs]
    each = lambda fn, *cols: [fn(*args) for args in zip(*cols)]
    stacked = lambda x: [_stack2(x[:, sl], m0) for sl in lanes]
    kt2, rt2, bh2, kh2, bbar2, kbar2, v2 = (stacked(x) for x in (kt, rt, bh, kh, bbar, kbar, v))

    aa = each(lambda a_, b_, c_, d_: _mm_nt(jnp.concatenate([a_, b_], axis=0), jnp.concatenate([c_, d_], axis=0)),
              kt2, rt2, bh2, kh2)
    lp = [jnp.where(strict, x[0:2 * c, 0:2 * c], 0.0) for x in aa]
    akp = [jnp.where(strict, x[0:2 * c, 2 * c:4 * c], 0.0) for x in aa]
    arb = [jnp.where(incl, x[2 * c:4 * c, 0:2 * c], 0.0) for x in aa]
    ark = [jnp.where(incl, x[2 * c:4 * c, 2 * c:4 * c], 0.0) for x in aa]

    ld = [jnp.where(blk, x, 0.0) for x in lp]
    lo_ = each(lambda x, y_: x - y_, lp, ld)
    p2 = each(_mm, ld, ld)
    p4 = each(_mm, p2, p2)
    p8 = each(_mm, p4, p4)
    e1 = each(lambda a_, b_, m: a_ - b_ - m, p2, ld, each(_mm, ld, p2))
    e2 = each(lambda a_, b_, m: a_ + b_ + m, e1, p4, each(_mm, e1, p4))
    dm = each(lambda a_, b_, m: a_ + b_ + m, e2, p8, each(_mm, e2, p8))
    n1 = each(lambda a_, m: a_ + m, lo_, each(_mm, dm, lo_))
    n2 = each(_mm, n1, n1)
    f = each(lambda a_, b_, m: a_ - b_ - m, n2, n1, each(_mm, n1, n2))
    tm = each(lambda a_, b_, m: a_ + b_ + m, f, dm, each(_mm, f, dm))

    akv = each(_mm, akp, v2)
    uu = each(lambda a_, b_: jnp.concatenate([a_, b_], axis=1), kt2, akv)
    pq = each(lambda a_, m: a_ + m, uu, each(_mm, tm, uu))
    rbpq = each(_mm, arb, pq)
    ry = each(lambda a_, m: a_ - m[:, 0:LANES], rt2, rbpq)
    y0 = each(lambda m, n_: m - n_[:, LANES:2 * LANES], each(_mm, ark, v2), rbpq)
    btpq = each(_mm_tn, bbar2, pq)
    g2 = [jnp.where(eye, gtot[:, sl], 0.0) - m[:, 0:LANES] for sl, m in zip(lanes, btpq)]
    h2 = each(lambda m, n_: m - n_[:, LANES:2 * LANES], each(_mm_tn, kbar2, v2), btpq)

    st = [s_ref[0, p] for p in pairs]
    y2 = each(lambda a_, b_, o: _mm(a_, b_) + o, ry, st, y0)
    s_new = each(lambda a_, b_, o: _mm(a_, b_) + o, g2, st, h2)
    for p in pairs:
        s_ref[0, p] = s_new[p]
    y = jnp.concatenate([x[0:c] + x[c:2 * c] for x in y2], axis=-1)

    inv_n = 1.0 / RW_HEAD
    mean = _segsum(y, seg) * inv_n
    yc = y - mean
    var = _segsum(yc * yc, seg) * inv_n
    yn = yc * lax.rsqrt(var + GN_EPS) * gain_ref[...] + bias_ref[...]
    bonus = _segsum(r * k * rk_ref[...], seg)
    yb_ref[...] = yn + bonus * v


def seq_mixers(proj, shift_rkv, shift_lora, pool_past, s0_blk, lw, *, prompt_batch, prompt_len, sample_len):
    c = CHUNK
    n_seq = shift_rkv.shape[0]
    tiles_p = prompt_len // c
    tiles_s = sample_len // c
    prompt_tiles = prompt_batch * tiles_p
    m = proj.shape[0]
    n_tiles = m // c
    seq = lambda i: jnp.where(i < prompt_tiles, i // tiles_p, prompt_batch + (i - prompt_tiles) // tiles_s)
    col = lambda width, cb: pl.BlockSpec((c, width), lambda i: (i, cb))
    per_seq = lambda *shape: pl.BlockSpec((1,) + shape, lambda i: (seq(i),) + tuple(0 for _ in shape))
    full = lambda *shape: pl.BlockSpec(shape, lambda i: tuple(0 for _ in shape))
    tri = (jnp.arange(c)[:, None] >= jnp.arange(c)[None, :]).astype(BF16)
    row = lambda v: v.reshape(1, -1)
    return pl.pallas_call(
        functools.partial(_seq_kernel, prompt_tiles=prompt_tiles, tiles_p=tiles_p, tiles_s=tiles_s),
        out_shape=(jax.ShapeDtypeStruct((m, POOL_WIDTH), BF16),
                   jax.ShapeDtypeStruct((m, RW_WIDTH), F32),
                   jax.ShapeDtypeStruct((m, RW_WIDTH), F32),
                   jax.ShapeDtypeStruct((n_seq, 3, RW_WIDTH), F32),
                   jax.ShapeDtypeStruct((n_seq, 1, LORA_PAD), F32),
                   jax.ShapeDtypeStruct((n_seq, POOL_PAD, POOL_WIDTH), F32),
                   jax.ShapeDtypeStruct((n_seq, PAIRS, LANES, LANES), F32)),
        grid=(n_tiles,),
        in_specs=[
            col(RW_WIDTH, 0), col(RW_WIDTH, 1), col(RW_WIDTH, 2), col(LORA_PAD, COL_LORA // LORA_PAD),
            col(POOL_WIDTH, COL_POOL // POOL_WIDTH),
            per_seq(3, RW_WIDTH), per_seq(1, LORA_PAD), per_seq(POOL_PAD, POOL_WIDTH), per_seq(PAIRS, LANES, LANES),
            full(3, RW_WIDTH), full(1, LORA_PAD),
            full(1, RW_WIDTH), full(1, RW_WIDTH), full(1, RW_WIDTH), full(1, RW_WIDTH),
            full(LANES, RW_WIDTH), full(LANES, RW_WIDTH), full(LORA_GATE_PAD, RW_WIDTH),
            full(POOL_GROUPS, POOL_GC, POOL_GC), full(1, POOL_WIDTH),
            full(1, RW_WIDTH), full(1, RW_WIDTH), full(1, RW_WIDTH),
            full(LANES, LANES), full(c, c),
        ],
        out_specs=(col(POOL_WIDTH, 0), col(RW_WIDTH, 0), col(RW_WIDTH, 0),
                   per_seq(3, RW_WIDTH), per_seq(1, LORA_PAD), per_seq(POOL_PAD, POOL_WIDTH),
                   per_seq(PAIRS, LANES, LANES)),
        scratch_shapes=[pltpu.VMEM((8, RW_WIDTH), F32), pltpu.VMEM((8, LORA_PAD), F32),
                        pltpu.VMEM((POOL_PAD, POOL_WIDTH), F32)],
        compiler_params=_params(("arbitrary",)),
        name="seq_mixers",
    )(proj, proj, proj, proj, proj, shift_rkv, shift_lora, pool_past, s0_blk,
      lw["mu_rkv"], lw["mu_lora"], lw["w0"], lw["a0"], lw["k_k"], lw["k_a"], lw["wd"], lw["wa"], lw["wg"],
      lw["pool_w"], row(lw["pool_scale"]), row(lw["gn_gain"]), row(lw["gn_bias"]), row(lw["r_k"]), lw["seg"], tri)


def _state_to_blocks(s0):
    b = s0.shape[0]
    st = jnp.swapaxes(s0, -1, -2).reshape(b, PAIRS, 2, RW_HEAD, RW_HEAD)
    z = jnp.zeros_like(st[:, :, 0])
    top = jnp.concatenate([st[:, :, 0], z], axis=-1)
    bottom = jnp.concatenate([z, st[:, :, 1]], axis=-1)
    return jnp.concatenate([top, bottom], axis=-2)


def _blocks_to_state(sb):
    b = sb.shape[0]
    d = jnp.stack([sb[:, :, :RW_HEAD, :RW_HEAD], sb[:, :, RW_HEAD:, RW_HEAD:]], axis=2)
    return jnp.swapaxes(d.reshape(b, RW_HEADS, RW_HEAD, RW_HEAD), -1, -2)


def _stacked_weights(w_in, pool_w, w_decay_up, w_aaa_up, w_gate_up, proj_pool, proj_rwkv, w_out,
                     w_ffn_gate, w_ffn_up, w_ffn_down, w_ple_gate, w_ple_proj):
    c_rw = POOL_WIDTH
    c_lora = POOL_WIDTH + 3 * RW_WIDTH
    c_gp = POOL_WIDTH + RW_PROJ
    w_t = jnp.swapaxes(w_in, 1, 2)
    w_in_t = jnp.concatenate([
        w_t[:, c_rw:c_lora], w_t[:, :c_rw], w_t[:, c_lora:c_gp],
        jnp.zeros((DEPTH, LORA_PAD - LORA, D_MODEL), F32), w_t[:, c_gp:]], axis=1).astype(BF16)
    zpad = lambda rows: jnp.zeros((DEPTH, rows, RW_WIDTH), F32)
    return dict(
        w_in=w_in_t, pool_w=pool_w.astype(BF16),
        wd=jnp.concatenate([w_decay_up, zpad(LANES - DECAY_LORA)], axis=1).astype(BF16),
        wa=jnp.concatenate([zpad(DECAY_LORA), w_aaa_up], axis=1).astype(BF16),
        wg=jnp.concatenate([w_gate_up, zpad(LORA_GATE_PAD - GATE_LORA)], axis=1).astype(BF16),
        proj_pool=proj_pool.astype(BF16), proj_rwkv=proj_rwkv.astype(BF16), w_out=w_out.astype(BF16),
        w_ffn_gate=w_ffn_gate.astype(BF16), w_ffn_up=w_ffn_up.astype(BF16), w_ffn_down=w_ffn_down.astype(BF16),
        w_ple_gate=w_ple_gate.astype(BF16), w_ple_proj=w_ple_proj.astype(BF16))


def kernel(x_prompt, x_sample, state_shift, state_pool, state_wkv, p_prompt, p_sample, norm_mix, w_in, mu_shift, pool_w, pool_scale, w0, w_decay_up, a0, w_aaa_up, w_gate_up, k_k, k_a, r_k, gn_gain, gn_bias, proj_pool, proj_rwkv, w_out, norm_ffn, w_ffn_gate, w_ffn_up, w_ffn_down, norm_ple, w_ple_gate, w_ple_proj, norm_final):
    bp, tp, _ = x_prompt.shape
    bs, ts, _ = x_sample.shape
    mp = bp * tp
    ms = bs * ts
    x = jnp.concatenate([x_prompt.reshape(mp, D_MODEL), x_sample.reshape(ms, D_MODEL)], axis=0)
    pp_tok = p_prompt.reshape(DEPTH, mp, PLE_DIM)
    ps_tok = p_sample.reshape(DEPTH, ms, PLE_DIM)
    sw = _stacked_weights(w_in, pool_w, w_decay_up, w_aaa_up, w_gate_up, proj_pool, proj_rwkv, w_out,
                          w_ffn_gate, w_ffn_up, w_ffn_down, w_ple_gate, w_ple_proj)
    lane = jnp.arange(LANES) // RW_HEAD
    seg = (lane[:, None] == lane[None, :]).astype(BF16)
    lead = lambda a: jnp.concatenate([jnp.zeros((DEPTH, bp) + a.shape[2:], a.dtype), a], axis=1)
    shift_all = lead(state_shift)
    shift_rkv = shift_all[:, :, :3 * RW_WIDTH].reshape(DEPTH, bp + bs, 3, RW_WIDTH)
    shift_lora = jnp.pad(shift_all[:, :, 3 * RW_WIDTH:], ((0, 0), (0, 0), (0, LORA_PAD - LORA)))
    shift_lora = shift_lora.reshape(DEPTH, bp + bs, 1, LORA_PAD)
    pool_all = jnp.pad(lead(state_pool), ((0, 0), (0, 0), (POOL_PAD - POOL_PAST, 0), (0, 0)))
    wkv_all = lead(state_wkv)
    row = lambda v: v.reshape(1, -1)
    new_shift, new_pool, new_wkv = [], [], []
    for i in range(DEPTH):
        mu = mu_shift[i]
        lw = dict(
            mu_rkv=mu[:3 * RW_WIDTH].reshape(3, RW_WIDTH),
            mu_lora=jnp.pad(mu[3 * RW_WIDTH:], (0, LORA_PAD - LORA)).reshape(1, LORA_PAD),
            pool_w=sw["pool_w"][i], pool_scale=pool_scale[i],
            w0=row(w0[i]), a0=row(a0[i]), k_k=row(k_k[i]), k_a=row(k_a[i]),
            wd=sw["wd"][i], wa=sw["wa"][i], wg=sw["wg"][i], seg=seg,
            r_k=r_k[i].reshape(RW_WIDTH), gn_gain=gn_gain[i], gn_bias=gn_bias[i])
        proj, gates = in_proj(x, norm_mix[i], sw["w_in"], i)
        ya, yb, g, ns_rkv, ns_lora, pool_rows, s_blk = seq_mixers(
            proj, shift_rkv[i], shift_lora[i], pool_all[i], _state_to_blocks(wkv_all[i]), lw,
            prompt_batch=bp, prompt_len=tp, sample_len=ts)
        new_shift.append(jnp.concatenate([ns_rkv.reshape(bp + bs, 3 * RW_WIDTH), ns_lora[:, 0, :LORA]], axis=1))
        new_pool.append(pool_rows[:, POOL_PAD - POOL_PAST:])
        new_wkv.append(_blocks_to_state(s_blk))
        x = merge_out(ya, yb, g, gates, x, sw["proj_pool"], sw["proj_rwkv"], sw["w_out"], i)
        x = ffn(x, norm_ffn[i], sw["w_ffn_gate"], sw["w_ffn_up"], sw["w_ffn_down"], i)
        if i < DEPTH - 1:
            x = ple_update(x, norm_ple[i], sw["w_ple_gate"], pp_tok, ps_tok, sw["w_ple_proj"], i)
    y_prompt, y_sample = ple_final(x, norm_ple[DEPTH - 1], sw["w_ple_gate"], pp_tok, ps_tok, sw["w_ple_proj"],
                                   norm_final, DEPTH - 1)
    y_prompt = y_prompt.reshape(bp, tp, D_MODEL)
    y_sample = y_sample.reshape(bs, ts, D_MODEL)
    new_shift, new_pool, new_wkv = jnp.stack(new_shift), jnp.stack(new_pool), jnp.stack(new_wkv)
    return (y_prompt, y_sample, new_shift[:, :bp], new_pool[:, :bp], new_wkv[:, :bp],
            new_shift[:, bp:], new_pool[:, bp:], new_wkv[:, bp:])
```
